```python
import jax, jax.numpy as jnp
from jax import lax
import numpy as np

D_MODEL = 1024
BATCH = 8
SEQ = 4096
DEPTH = 2

HEAD_DIM = 64
N_MIXERS = 4
GROUP_WIDTH = D_MODEL // N_MIXERS
N_GROUP_HEADS = GROUP_WIDTH // HEAD_DIM
MIX_WIDTH = N_MIXERS * GROUP_WIDTH
N_IN_SLICES = 13
IN_WIDTH = N_IN_SLICES * GROUP_WIDTH
DILATED_PAIRS = ((128, 1), (512, 4), (2048, 16))
ATT_BLOCK = 128
SGU_CHUNK = 128
POOL_SIZES = (2, 4, 8, 16)
POOL_CH = GROUP_WIDTH // len(POOL_SIZES)
RET_CHUNK = 128
NORM_EPS = 1e-6

kernel_name = "hymba_hybrid_dilated_sgu_pool_retention"


def _rms_norm(x, g):
    xf = x.astype(jnp.float32)
    y = xf * lax.rsqrt(jnp.mean(xf * xf, axis=-1, keepdims=True) + NORM_EPS)
    return (y * g.astype(jnp.float32)).astype(x.dtype)


def _layer_norm(x, g):
    xf = x.astype(jnp.float32)
    mu = jnp.mean(xf, axis=-1, keepdims=True)
    var = jnp.mean(jnp.square(xf - mu), axis=-1, keepdims=True)
    return ((xf - mu) * lax.rsqrt(var + NORM_EPS) * g.astype(jnp.float32)).astype(x.dtype)


def _alibi_slopes(n):
    return jnp.exp2(-8.0 * (jnp.arange(n, dtype=jnp.float32) + 1.0) / n)


def _dilated_branch(q, k, v, window, dil, slopes):
    b, h, s, hd = q.shape
    span = dil * ATT_BLOCK
    s_pad = -(-s // span) * span
    L = s_pad // dil
    nb = L // ATT_BLOCK

    def stride_blocks(t):
        t = jnp.pad(t, ((0, 0), (0, 0), (0, s_pad - s), (0, 0))).reshape(b, h, L, dil, hd)
        return jnp.swapaxes(t, 2, 3).reshape(b, h, dil, nb, ATT_BLOCK, hd)

    def with_prev(t):
        prev = jnp.pad(t[:, :, :, :-1], ((0, 0), (0, 0), (0, 0), (1, 0), (0, 0), (0, 0)))
        return jnp.concatenate([prev, t], axis=4)

    qb = stride_blocks(q)
    kc = with_prev(stride_blocks(k))
    vc = with_prev(stride_blocks(v))
    scores = jnp.einsum('bhrnqd,bhrnkd->bhrnqk', qb, kc).astype(jnp.float32) * (hd ** -0.5)
    qi = jnp.arange(ATT_BLOCK)[:, None]
    ki = jnp.arange(2 * ATT_BLOCK)[None, :]
    dist = qi + ATT_BLOCK - ki
    band = (dist >= 0) & (dist <= window // dil)
    valid = (jnp.arange(nb)[:, None] * ATT_BLOCK + ki - ATT_BLOCK) >= 0
    mask = band[None, :, :] & valid[:, None, :]
    bias = -slopes[:, None, None, None, None] * (dist * dil).astype(jnp.float32)
    scores = jnp.where(mask, scores + bias, -jnp.inf)
    m = jnp.max(scores, axis=-1, keepdims=True)
    p = jnp.exp(scores - m)
    den = jnp.sum(p, axis=-1, keepdims=True)
    o = jnp.einsum('bhrnqk,bhrnkd->bhrnqd', (p / den).astype(v.dtype), vc)
    lse = (m + jnp.log(den))[..., 0]

    def unstride(t):
        tail = t.shape[5:]
        t = t.reshape(b, h, dil, L, *tail)
        return jnp.swapaxes(t, 2, 3).reshape(b, h, s_pad, *tail)[:, :, :s]

    return unstride(o), unstride(lse)


def _dilated_mixture(q, k, v):
    slopes = _alibi_slopes(q.shape[1])
    outs, lses = [], []
    for window, dil in DILATED_PAIRS:
        o, lse = _dilated_branch(q, k, v, window, dil, slopes)
        outs.append(o)
        lses.append(lse)
    w = jax.nn.softmax(jnp.stack(lses), axis=0)
    o = jnp.einsum('gbhs,gbhsd->bhsd', w.astype(v.dtype), jnp.stack(outs))
    b, h, s, hd = o.shape
    return o.transpose(0, 2, 1, 3).reshape(b, s, h * hd)


def _spatial_gating(u, v, norm_g, w_s, b_s):
    b, s, _ = u.shape
    v = _layer_norm(v, norm_g)
    nc = s // SGU_CHUNK
    v = v.reshape(b, nc, SGU_CHUNK, N_GROUP_HEADS, HEAD_DIM)
    causal = jnp.tril(jnp.ones((SGU_CHUNK, SGU_CHUNK), dtype=bool))
    w = jnp.where(causal, w_s, 0.0)
    mixed = jnp.einsum('gts,bcsgd->bctgd', w, v) + b_s.T[:, :, None]
    return u * mixed.reshape(b, s, GROUP_WIDTH)


def _multiscale_pool(xc, pool_w, pool_scale):
    b, s, _ = xc.shape
    xg = xc.reshape(b, s, len(POOL_SIZES), POOL_CH)
    csum = jnp.pad(lax.cumsum(xg.astype(jnp.float32), axis=1), ((0, 0), (1, 0), (0, 0), (0, 0)))
    t = jnp.arange(s)
    pooled = []
    for g, p in enumerate(POOL_SIZES):
        lo = jnp.maximum(t + 1 - p, 0)
        win_sum = csum[:, 1:, g] - csum[:, lo, g]
        cnt = jnp.minimum(t + 1, p).astype(jnp.float32)
        pooled.append(win_sum / cnt[None, :, None])
    pooled = jnp.stack(pooled, axis=2).astype(xc.dtype) - xg
    y = jnp.einsum('bsgc,gcd->bsgd', pooled, pool_w)
    return y.reshape(b, s, GROUP_WIDTH) * pool_scale


def _retention(q, k, v, norm_g):
    b, s, _ = q.shape
    H, C = N_GROUP_HEADS, RET_CHUNK
    n = s // C

    def chunks(t):
        return t.reshape(b, n, C, H, HEAD_DIM).transpose(0, 3, 1, 2, 4)

    q, k, v = chunks(q), chunks(k) * (HEAD_DIM ** -0.5), chunks(v)
    log_g = jnp.log(1.0 - jnp.exp2(-5.0 - jnp.arange(H, dtype=jnp.float32)))
    i = jnp.arange(C, dtype=jnp.float32)
    diff = i[:, None] - i[None, :]
    decay = jnp.where(diff >= 0, jnp.exp(log_g[:, None, None] * jnp.maximum(diff, 0.0)), 0.0)
    zeta = jnp.exp(log_g[:, None] * (C - 1 - i))
    xi = jnp.exp(log_g[:, None] * (i + 1))
    chunk_decay = jnp.exp(log_g * C)
    inner = jnp.einsum('bhnid,bhnjd->bhnij', q, k) * decay[:, None].astype(q.dtype)
    inner = jnp.einsum('bhnij,bhnje->bhnie', inner, v)
    kv = jnp.einsum('bhnjd,bhnje->nbhde', k * zeta[:, None, :, None].astype(k.dtype), v).astype(jnp.float32)

    def step(state, kv_n):
        return state * chunk_decay[None, :, None, None] + kv_n, state

    _, prev = lax.scan(step, jnp.zeros_like(kv[0]), kv)
    cross = jnp.einsum('bhnid,nbhde->bhnie', q * xi[:, None, :, None].astype(q.dtype), prev.astype(q.dtype))
    of = (inner + cross).astype(jnp.float32)
    mu = jnp.mean(of, axis=-1, keepdims=True)
    var = jnp.mean(jnp.square(of - mu), axis=-1, keepdims=True)
    on = (of - mu) * lax.rsqrt(var + NORM_EPS)
    on = on.transpose(0, 2, 3, 1, 4).reshape(b, s, GROUP_WIDTH)
    return (on * norm_g.astype(jnp.float32)).astype(v.dtype)


def _layer(x, pre_g, w_in, sgu_g, sgu_w, sgu_b, pool_w, pool_scale, ret_g, w_out, post_g):
    b, s, _ = x.shape
    h = _rms_norm(x, pre_g)
    proj = jnp.einsum('bsd,de->bse', h, w_in)
    aq, ak, av, ag, bu, bv, bg, cx, cg, dq, dk, dv, dg = jnp.split(proj, N_IN_SLICES, axis=-1)

    def heads(t):
        return t.reshape(b, s, N_GROUP_HEADS, HEAD_DIM).transpose(0, 2, 1, 3)

    ya = _dilated_mixture(heads(aq), heads(ak), heads(av))
    yb = _spatial_gating(bu, bv, sgu_g, sgu_w, sgu_b)
    yc = _multiscale_pool(cx, pool_w, pool_scale)
    yd = _retention(dq, dk, dv, ret_g)
    y = jnp.concatenate([ya * jax.nn.silu(ag), yb * jax.nn.silu(bg),
                         yc * jax.nn.silu(cg), yd * jax.nn.silu(dg)], axis=-1)
    y = jnp.einsum('bse,ed->bsd', y, w_out)
    return x + _rms_norm(y, post_g).astype(x.dtype)


def setup_inputs(seed: int = 0) -> dict:
    key = jax.random.key(seed)
    ks = jax.random.split(key, 12)
    f32 = jnp.float32
    nrm = lambda k, shape: jax.random.normal(k, shape, f32)
    return {
        "x": nrm(ks[0], (BATCH, SEQ, D_MODEL)),
        "pre_g": 1.0 + 0.05 * nrm(ks[1], (DEPTH, D_MODEL)),
        "w_in": nrm(ks[2], (DEPTH, D_MODEL, IN_WIDTH)) * D_MODEL ** -0.5,
        "sgu_g": 1.0 + 0.05 * nrm(ks[3], (DEPTH, GROUP_WIDTH)),
        "sgu_w": nrm(ks[4], (DEPTH, N_GROUP_HEADS, SGU_CHUNK, SGU_CHUNK)) * SGU_CHUNK ** -0.5,
        "sgu_b": 1.0 + 0.05 * nrm(ks[5], (DEPTH, N_GROUP_HEADS, SGU_CHUNK)),
        "pool_w": nrm(ks[6], (DEPTH, len(POOL_SIZES), POOL_CH, POOL_CH)) * POOL_CH ** -0.5,
        "pool_scale": 1.0 + 0.1 * nrm(ks[7], (DEPTH, GROUP_WIDTH)),
        "ret_g": 1.0 + 0.05 * nrm(ks[8], (DEPTH, GROUP_WIDTH)),
        "w_out": nrm(ks[9], (DEPTH, MIX_WIDTH, D_MODEL)) * MIX_WIDTH ** -0.5,
        "post_g": 1.0 + 0.05 * nrm(ks[10], (DEPTH, D_MODEL)),
    }


def reference(x, pre_g, w_in, sgu_g, sgu_w, sgu_b, pool_w, pool_scale, ret_g, w_out, post_g):
    for l in range(DEPTH):
        x = _layer(x, pre_g[l], w_in[l], sgu_g[l], sgu_w[l], sgu_b[l], pool_w[l],
                   pool_scale[l], ret_g[l], w_out[l], post_g[l])
    return x
```

```python
import functools
import math

import jax
import jax.numpy as jnp
from jax import lax
from jax.experimental import pallas as pl
from jax.experimental.pallas import tpu as pltpu

F32 = jnp.float32
BF16 = jnp.bfloat16

HEAD_DIM = 64
N_HEADS = 4
GROUP_WIDTH = HEAD_DIM * N_HEADS
LANE = 128
N_SLABS = GROUP_WIDTH // LANE
CHUNK = 128
DILATIONS = (1, 4, 16)
POOL_SIZES = (2, 4, 8, 16)
POOL_HIST = 32
NORM_EPS = 1e-6
IN_TILE = 512
ATT_TILE = CHUNK * DILATIONS[-1]
VMEM_LIMIT = 56 * 1024 * 1024


def _silu(x):
    return x / (1.0 + jnp.exp(-x))


def _lane_head(width=GROUP_WIDTH):
    return lax.broadcasted_iota(jnp.int32, (1, width), 1) // HEAD_DIM


def _dot(a, b):
    return jnp.dot(a, b, preferred_element_type=F32)


def _dot_nt(a, b):
    return lax.dot_general(a, b, (((1,), (1,)), ((), ())), preferred_element_type=F32)


def _dot_tn(a, b):
    return lax.dot_general(a, b, (((0,), (0,)), ((), ())), preferred_element_type=F32)


def _seg_mean(a, seg):
    hi = a.astype(BF16)
    lo = (a - hi.astype(F32)).astype(BF16)
    return _dot(hi, seg) + _dot(lo, seg)


def _inproj_mix_kernel(x_ref, preg_ref, win_ref, sgug_ref, sguw_ref, sgub_ref, poolw_ref, pools_ref,
                       retg_ref, decay_ref, zeta_ref, xi_ref, cdec_ref,
                       q_ref, k_ref, v_ref, ga_ref, y_ref,
                       e0, s2, s4, s8, state, vn_s, mix_s, rq_s, rk_s, rv_s, of_s):
    n = pl.program_id(1)
    tile = x_ref.shape[1]
    n_chunks = tile // CHUNK
    gw = GROUP_WIDTH

    @pl.when(n == 0)
    def _():
        e0[0:POOL_HIST, :] = jnp.zeros((POOL_HIST, gw), F32)
        state[...] = jnp.zeros_like(state)

    x = x_ref[0]
    ms = jnp.mean(x * x, axis=-1, keepdims=True)
    h = (x * lax.rsqrt(ms + NORM_EPS) * preg_ref[...]).astype(BF16)

    lane_head = _lane_head()

    pa = _dot(h, win_ref[:, 0:4 * gw])
    qa = pa[:, 0:gw] * (HEAD_DIM ** -0.5)
    ka = pa[:, gw:2 * gw]
    va = pa[:, 2 * gw:3 * gw]
    for s in range(N_SLABS):
        q_ref[0, s] = qa[:, s * LANE:(s + 1) * LANE]
        k_ref[0, s] = ka[:, s * LANE:(s + 1) * LANE]
        v_ref[0, s] = va[:, s * LANE:(s + 1) * LANE]
    ga_ref[0] = _silu(pa[:, 3 * gw:4 * gw]).astype(BF16)

    pb = _dot(h, win_ref[:, 4 * gw:7 * gw])
    bu = pb[:, 0:gw]
    bv = pb[:, gw:2 * gw]
    bg = pb[:, 2 * gw:3 * gw]
    mu = jnp.mean(bv, axis=-1, keepdims=True)
    dv = bv - mu
    var = jnp.mean(dv * dv, axis=-1, keepdims=True)
    vn_s[...] = (dv * lax.rsqrt(var + NORM_EPS) * sgug_ref[...]).astype(BF16)
    causal = (lax.broadcasted_iota(jnp.int32, (CHUNK, CHUNK), 0)
              >= lax.broadcasted_iota(jnp.int32, (CHUNK, CHUNK), 1))
    w_tril = [jnp.where(causal, sguw_ref[g], 0.0).astype(BF16) for g in range(N_HEADS)]

    def sgu_chunk(c, carry):
        r0 = pl.multiple_of(c * CHUNK, CHUNK)
        vc = vn_s[pl.ds(r0, CHUNK), :]
        acc = sgub_ref[...]
        for g in range(N_HEADS):
            acc = acc + _dot(w_tril[g], jnp.where(lane_head == g, vc, jnp.zeros_like(vc)))
        mix_s[pl.ds(r0, CHUNK), :] = acc
        return carry

    lax.fori_loop(0, n_chunks, sgu_chunk, 0)
    y_ref[0, :, 0:gw] = (bu * mix_s[...] * _silu(bg)).astype(BF16)

    pc = _dot(h, win_ref[:, 7 * gw:9 * gw])
    cx = pc[:, 0:gw]
    cg = pc[:, gw:2 * gw]
    hist = POOL_HIST
    e0[hist:hist + tile, :] = cx
    s2[8:hist + tile, :] = e0[8:hist + tile, :] + e0[7:hist + tile - 1, :]
    s4[16:hist + tile, :] = s2[16:hist + tile, :] + s2[14:hist + tile - 2, :]
    s8[24:hist + tile, :] = s4[24:hist + tile, :] + s4[20:hist + tile - 4, :]
    w16 = s8[hist:hist + tile, :] + s8[hist - 8:hist + tile - 8, :]
    lane = lax.broadcasted_iota(jnp.int32, (1, gw), 1)
    win = jnp.where(lane < 64, s2[hist:hist + tile, :],
                    jnp.where(lane < 128, s4[hist:hist + tile, :],
                              jnp.where(lane < 192, s8[hist:hist + tile, :], w16)))
    p_lane = jnp.where(lane < 64, POOL_SIZES[0],
                       jnp.where(lane < 128, POOL_SIZES[1],
                                 jnp.where(lane < 192, POOL_SIZES[2], POOL_SIZES[3])))
    t_glob = n * tile + lax.broadcasted_iota(jnp.int32, (tile, 1), 0)
    cnt = jnp.minimum(t_glob + 1, p_lane).astype(F32)
    pooled = win / cnt - cx
    yc = _dot(pooled.astype(BF16), poolw_ref[...]) * pools_ref[...]
    y_ref[0, :, gw:2 * gw] = (yc * _silu(cg)).astype(BF16)
    e0[0:hist, :] = e0[tile:tile + hist, :]

    pd = _dot(h, win_ref[:, 9 * gw:13 * gw])
    rq_s[...] = pd[:, 0:gw]
    rk_s[...] = pd[:, gw:2 * gw] * (HEAD_DIM ** -0.5)
    rv_s[...] = pd[:, 2 * gw:3 * gw]
    dg = pd[:, 3 * gw:4 * gw]
    row_head = lax.broadcasted_iota(jnp.int32, (gw, 1), 0) // HEAD_DIM
    same_head = row_head == lane_head
    seg = jnp.where(same_head, 1.0 / HEAD_DIM, 0.0).astype(BF16)

    def ret_chunk(c, carry):
        r0 = pl.multiple_of(c * CHUNK, CHUNK)
        qc = rq_s[pl.ds(r0, CHUNK), :]
        kc = rk_s[pl.ds(r0, CHUNK), :]
        vb = rv_s[pl.ds(r0, CHUNK), :].astype(BF16)
        qb = qc.astype(BF16)
        kb = kc.astype(BF16)
        st = state[...]
        of = _dot((qc * xi_ref[...]).astype(BF16), st.astype(BF16))
        for hh in range(N_HEADS):
            in_head = lane_head == hh
            sc = _dot_nt(jnp.where(in_head, qb, jnp.zeros_like(qb)), kb) * decay_ref[hh]
            of = of + _dot(sc.astype(BF16), jnp.where(in_head, vb, jnp.zeros_like(vb)))
        of_s[pl.ds(r0, CHUNK), :] = of
        kv = _dot_tn((kc * zeta_ref[...]).astype(BF16), vb)
        state[...] = st * cdec_ref[...] + jnp.where(same_head, kv, 0.0)
        return carry

    lax.fori_loop(0, n_chunks, ret_chunk, 0)
    of = of_s[...]
    mu_h = _seg_mean(of, seg)
    dh = of - mu_h
    var_h = _seg_mean(dh * dh, seg)
    yd = dh * lax.rsqrt(var_h + NORM_EPS) * retg_ref[...]
    y_ref[0, :, 2 * gw:3 * gw] = (yd * _silu(dg)).astype(BF16)


def _inproj_mix(x, pre_g, w_in, sgu_g, sgu_w, sgu_bias, pool_w_bd, pool_scale, ret_g,
                decay, zeta, xi, cdec):
    b, s, d = x.shape
    tile = IN_TILE
    gw = GROUP_WIDTH
    const2 = lambda bi, ni: (0, 0)
    const3 = lambda bi, ni: (0, 0, 0)
    slab_spec = pl.BlockSpec((1, N_SLABS, tile, LANE), lambda bi, ni: (bi, 0, ni, 0))
    slab_shape = jax.ShapeDtypeStruct((b, N_SLABS, s, LANE), F32)
    return pl.pallas_call(
        _inproj_mix_kernel,
        grid=(b, s // tile),
        in_specs=[
            pl.BlockSpec((1, tile, d), lambda bi, ni: (bi, ni, 0)),
            pl.BlockSpec((1, d), const2),
            pl.BlockSpec(w_in.shape, const2),
            pl.BlockSpec((1, gw), const2),
            pl.BlockSpec(sgu_w.shape, const3),
            pl.BlockSpec((CHUNK, gw), const2),
            pl.BlockSpec((gw, gw), const2),
            pl.BlockSpec((1, gw), const2),
            pl.BlockSpec((1, gw), const2),
            pl.BlockSpec(decay.shape, const3),
            pl.BlockSpec((CHUNK, gw), const2),
            pl.BlockSpec((CHUNK, gw), const2),
            pl.BlockSpec((1, gw), const2),
        ],
        out_specs=[
            slab_spec, slab_spec, slab_spec,
            pl.BlockSpec((1, tile, gw), lambda bi, ni: (bi, ni, 0)),
            pl.BlockSpec((1, tile, 3 * gw), lambda bi, ni: (bi, ni, 0)),
        ],
        out_shape=[
            slab_shape, slab_shape, slab_shape,
            jax.ShapeDtypeStruct((b, s, gw), BF16),
            jax.ShapeDtypeStruct((b, s, 3 * gw), BF16),
        ],
        scratch_shapes=[
            pltpu.VMEM((tile + POOL_HIST, gw), F32),
            pltpu.VMEM((tile + POOL_HIST, gw), F32),
            pltpu.VMEM((tile + POOL_HIST, gw), F32),
            pltpu.VMEM((tile + POOL_HIST, gw), F32),
            pltpu.VMEM((gw, gw), F32),
            pltpu.VMEM((tile, gw), BF16),
            pltpu.VMEM((tile, gw), F32),
            pltpu.VMEM((tile, gw), F32),
            pltpu.VMEM((tile, gw), F32),
            pltpu.VMEM((tile, gw), F32),
            pltpu.VMEM((tile, gw), F32),
        ],
        compiler_params=pltpu.CompilerParams(
            dimension_semantics=("arbitrary", "arbitrary"), vmem_limit_bytes=VMEM_LIMIT),
        name="inproj_mix",
    )(x, pre_g, w_in, sgu_g, sgu_w, sgu_bias, pool_w_bd, pool_scale, ret_g, decay, zeta, xi, cdec)


def _dilated_attn_kernel(q_ref, k_ref, v_ref, ga_ref, o_ref, bias_s, ob_s, lb_s):
    bi = pl.program_id(0)
    qb = pl.program_id(1)
    base = qb * ATT_TILE
    lane_head = _lane_head()

    @pl.when((bi == 0) & (qb == 0))
    def _():
        qi = lax.broadcasted_iota(jnp.int32, (CHUNK, 2 * CHUNK), 0)
        ki = lax.broadcasted_iota(jnp.int32, (CHUNK, 2 * CHUNK), 1)
        dist = qi + CHUNK - ki
        band = (dist >= 0) & (dist <= CHUNK)
        distf = dist.astype(F32)
        for di, dil in enumerate(DILATIONS):
            for hh in range(N_HEADS):
                slope = 2.0 ** (-8.0 * (hh + 1) / N_HEADS)
                bias = -(slope * dil) * distf
                bias_s[0, di, hh] = jnp.where(band, bias, -jnp.inf)
                bias_s[1, di, hh] = jnp.where(band & (ki >= CHUNK), bias, -jnp.inf)

    def rows(ref, start, stride):
        idx = pl.ds(start, CHUNK) if stride == 1 else pl.ds(start, CHUNK, stride=stride)
        return jnp.concatenate([ref[0, s, idx, :] for s in range(N_SLABS)], axis=-1).astype(BF16)

    def attn_block(di, q_start, q_local, blk, first):
        dil = DILATIONS[di]
        span = dil * CHUNK
        cur = q_start
        prev = jnp.where(first, cur, cur - span)
        fi = first.astype(jnp.int32)
        qv = rows(q_ref, q_local, dil)
        kk = jnp.concatenate([rows(k_ref, prev, dil), rows(k_ref, cur, dil)], axis=0)
        vv = jnp.concatenate([rows(v_ref, prev, dil), rows(v_ref, cur, dil)], axis=0)
        o = jnp.zeros((CHUNK, GROUP_WIDTH), F32)
        lse = jnp.zeros((CHUNK, GROUP_WIDTH), F32)
        for hh in range(N_HEADS):
            in_head = lane_head == hh
            sc = _dot_nt(jnp.where(in_head, qv, jnp.zeros_like(qv)), kk) + bias_s[fi, di, hh]
            m = jnp.max(sc, axis=-1, keepdims=True)
            p = jnp.exp(sc - m)
            l = jnp.sum(p, axis=-1, keepdims=True)
            pv = _dot(p.astype(BF16), vv)
            o = jnp.where(in_head, pv / l, o)
            lse = jnp.where(in_head, m + jnp.log(l), lse)
        return o, lse

    def strided_branch(di, slot):
        dil = DILATIONS[di]
        blocks_per_class = ATT_TILE // (dil * CHUNK)

        def body(i, carry):
            r = i // blocks_per_class
            nb = i % blocks_per_class
            q_local = r + nb * (dil * CHUNK)
            first = (qb == 0) & (nb == 0)
            o, lse = attn_block(di, base + q_local, q_local, nb, first)
            idx = pl.ds(q_local, CHUNK, stride=dil)
            for s in range(N_SLABS):
                ob_s[slot, s, idx, :] = o[:, s * LANE:(s + 1) * LANE]
                lb_s[slot, s, idx, :] = lse[:, s * LANE:(s + 1) * LANE]
            return carry

        lax.fori_loop(0, dil * blocks_per_class, body, 0)

    strided_branch(1, 0)
    strided_branch(2, 1)

    def dense_body(nb, carry):
        q_local = pl.multiple_of(nb * CHUNK, CHUNK)
        first = (qb == 0) & (nb == 0)
        o1, l1 = attn_block(0, base + q_local, q_local, nb, first)
        idx = pl.ds(q_local, CHUNK)
        o4 = jnp.concatenate([ob_s[0, s, idx, :] for s in range(N_SLABS)], axis=-1)
        l4 = jnp.concatenate([lb_s[0, s, idx, :] for s in range(N_SLABS)], axis=-1)
        o16 = jnp.concatenate([ob_s[1, s, idx, :] for s in range(N_SLABS)], axis=-1)
        l16 = jnp.concatenate([lb_s[1, s, idx, :] for s in range(N_SLABS)], axis=-1)
        mx = jnp.maximum(jnp.maximum(l1, l4), l16)
        w1 = jnp.exp(l1 - mx)
        w4 = jnp.exp(l4 - mx)
        w16 = jnp.exp(l16 - mx)
        o = (w1 * o1 + w4 * o4 + w16 * o16) / (w1 + w4 + w16)
        o_ref[0, idx, :] = (o * ga_ref[0, idx, :].astype(F32)).astype(BF16)
        return carry

    lax.fori_loop(0, ATT_TILE // CHUNK, dense_body, 0)


def _dilated_attn(q, k, v, ga):
    b, _, s, _ = q.shape
    gw = GROUP_WIDTH
    n_branch = len(DILATIONS)
    return pl.pallas_call(
        _dilated_attn_kernel,
        grid=(b, s // ATT_TILE),
        in_specs=[
            pl.BlockSpec((1, N_SLABS, ATT_TILE, LANE), lambda bi, qi: (bi, 0, qi, 0)),
            pl.BlockSpec((1, N_SLABS, s, LANE), lambda bi, qi: (bi, 0, 0, 0)),
            pl.BlockSpec((1, N_SLABS, s, LANE), lambda bi, qi: (bi, 0, 0, 0)),
            pl.BlockSpec((1, ATT_TILE, gw), lambda bi, qi: (bi, qi, 0)),
        ],
        out_specs=pl.BlockSpec((1, ATT_TILE, gw), lambda bi, qi: (bi, qi, 0)),
        out_shape=jax.ShapeDtypeStruct((b, s, gw), BF16),
        scratch_shapes=[
            pltpu.VMEM((2, n_branch, N_HEADS, CHUNK, 2 * CHUNK), F32),
            pltpu.VMEM((n_branch - 1, N_SLABS, ATT_TILE, LANE), F32),
            pltpu.VMEM((n_branch - 1, N_SLABS, ATT_TILE, LANE), F32),
        ],
        compiler_params=pltpu.CompilerParams(
            dimension_semantics=("arbitrary", "arbitrary"), vmem_limit_bytes=VMEM_LIMIT),
        name="dilated_attn",
    )(q, k, v, ga)


def _outproj_kernel(x_ref, ya_ref, y_ref, wout_ref, postg_ref, o_ref):
    gw = GROUP_WIDTH
    y = _dot(ya_ref[0], wout_ref[0:gw, :]) + _dot(y_ref[0], wout_ref[gw:, :])
    ms = jnp.mean(y * y, axis=-1, keepdims=True)
    o_ref[0] = x_ref[0] + y * lax.rsqrt(ms + NORM_EPS) * postg_ref[...]


def _outproj(x, ya, ybcd, w_out, post_g):
    b, s, d = x.shape
    tile = IN_TILE
    gw = GROUP_WIDTH
    return pl.pallas_call(
        _outproj_kernel,
        grid=(b, s // tile),
        in_specs=[
            pl.BlockSpec((1, tile, d), lambda bi, ni: (bi, ni, 0)),
            pl.BlockSpec((1, tile, gw), lambda bi, ni: (bi, ni, 0)),
            pl.BlockSpec((1, tile, 3 * gw), lambda bi, ni: (bi, ni, 0)),
            pl.BlockSpec(w_out.shape, lambda bi, ni: (0, 0)),
            pl.BlockSpec((1, d), lambda bi, ni: (0, 0)),
        ],
        out_specs=pl.BlockSpec((1, tile, d), lambda bi, ni: (bi, ni, 0)),
        out_shape=jax.ShapeDtypeStruct(x.shape, x.dtype),
        compiler_params=pltpu.CompilerParams(
            dimension_semantics=("arbitrary", "arbitrary"), vmem_limit_bytes=VMEM_LIMIT),
        name="outproj",
    )(x, ya, ybcd, w_out, post_g)


def _retention_tables():
    log_g = jnp.log(1.0 - jnp.exp2(-5.0 - jnp.arange(N_HEADS, dtype=F32)))
    i = jnp.arange(CHUNK, dtype=F32)
    diff = i[:, None] - i[None, :]
    decay = jnp.where(diff >= 0, jnp.exp(log_g[:, None, None] * jnp.maximum(diff, 0.0)), 0.0)
    zeta = jnp.exp(log_g[:, None] * (CHUNK - 1 - i))
    xi = jnp.exp(log_g[:, None] * (i + 1))
    cdec = jnp.exp(log_g * CHUNK)
    per_lane = lambda t: jnp.repeat(t.T, HEAD_DIM, axis=1)
    return decay, per_lane(zeta), per_lane(xi), jnp.repeat(cdec, HEAD_DIM)[None, :]


def _layer(x, pre_g, w_in, sgu_g, sgu_w, sgu_b, pool_w, pool_scale, ret_g, w_out, post_g, tables):
    decay, zeta, xi, cdec = tables
    sgu_bias = jnp.repeat(sgu_b.T, HEAD_DIM, axis=1)
    pool_w_bd = jax.scipy.linalg.block_diag(*[pool_w[g] for g in range(len(POOL_SIZES))]).astype(BF16)
    q, k, v, ga, ybcd = _inproj_mix(
        x, pre_g[None, :], w_in.astype(BF16), sgu_g[None, :], sgu_w, sgu_bias, pool_w_bd,
        pool_scale[None, :], ret_g[None, :], decay, zeta, xi, cdec)
    ya = _dilated_attn(q, k, v, ga)
    return _outproj(x, ya, ybcd, w_out.astype(BF16), post_g[None, :])


def kernel(x, pre_g, w_in, sgu_g, sgu_w, sgu_b, pool_w, pool_scale, ret_g, w_out, post_g):
    tables = _retention_tables()
    for l in range(pre_g.shape[0]):
        x = _layer(x, pre_g[l], w_in[l], sgu_g[l], sgu_w[l], sgu_b[l], pool_w[l], pool_scale[l],
                   ret_g[l], w_out[l], post_g[l], tables)
    return x
```

```python
import functools
import math

import jax
import jax.numpy as jnp
from jax import lax
from jax.experimental import pallas as pl
from jax.experimental.pallas import tpu as pltpu

F32 = jnp.float32
BF16 = jnp.bfloat16

HEAD_DIM = 64
N_HEADS = 4
GROUP_WIDTH = HEAD_DIM * N_HEADS
LANE = 128
N_SLABS = GROUP_WIDTH // LANE
CHUNK = 128
DILATIONS = (1, 4, 16)
POOL_SIZES = (2, 4, 8, 16)
POOL_HIST = 32
NORM_EPS = 1e-6
IN_TILE = 512
ATT_TILE = CHUNK * DILATIONS[-1]
VMEM_LIMIT = 56 * 1024 * 1024


def _silu(x):
    return x / (1.0 + jnp.exp(-x))


def _lane_head(width=GROUP_WIDTH):
    return lax.broadcasted_iota(jnp.int32, (1, width), 1) // HEAD_DIM


def _dot(a, b):
    return jnp.dot(a, b, preferred_element_type=F32)


def _dot_nt(a, b):
    return lax.dot_general(a, b, (((1,), (1,)), ((), ())), preferred_element_type=F32)


def _dot_tn(a, b):
    return lax.dot_general(a, b, (((0,), (0,)), ((), ())), preferred_element_type=F32)


def _seg_mean(a, seg):
    hi = a.astype(BF16)
    lo = (a - hi.astype(F32)).astype(BF16)
    return _dot(hi, seg) + _dot(lo, seg)


def _inproj_mix_kernel(x_ref, preg_ref, win_ref, sgug_ref, sguw_ref, sgub_ref, poolw_ref, pools_ref,
                       retg_ref, decay_ref, zeta_ref, xi_ref, cdec_ref,
                       q_ref, k_ref, v_ref, ga_ref, y_ref,
                       e0, s2, s4, s8, state, vn_s, mix_s, rq_s, rk_s, rv_s, of_s):
    n = pl.program_id(1)
    tile = x_ref.shape[1]
    n_chunks = tile // CHUNK
    gw = GROUP_WIDTH

    @pl.when(n == 0)
    def _():
        e0[0:POOL_HIST, :] = jnp.zeros((POOL_HIST, gw), F32)
        state[...] = jnp.zeros_like(state)

    x = x_ref[0]
    ms = jnp.mean(x * x, axis=-1, keepdims=True)
    h = (x * lax.rsqrt(ms + NORM_EPS) * preg_ref[...]).astype(BF16)

    lane_head = _lane_head()

    pa = _dot(h, win_ref[:, 0:4 * gw])
    qa = pa[:, 0:gw] * (HEAD_DIM ** -0.5)
    ka = pa[:, gw:2 * gw]
    va = pa[:, 2 * gw:3 * gw]
    for s in range(N_SLABS):
        q_ref[0, s] = qa[:, s * LANE:(s + 1) * LANE]
        k_ref[0, s] = ka[:, s * LANE:(s + 1) * LANE]
        v_ref[0, s] = va[:, s * LANE:(s + 1) * LANE]
    ga_ref[0] = _silu(pa[:, 3 * gw:4 * gw]).astype(BF16)

    pb = _dot(h, win_ref[:, 4 * gw:7 * gw])
    bu = pb[:, 0:gw]
    bv = pb[:, gw:2 * gw]
    bg = pb[:, 2 * gw:3 * gw]
    mu = jnp.mean(bv, axis=-1, keepdims=True)
    dv = bv - mu
    var = jnp.mean(dv * dv, axis=-1, keepdims=True)
    vn_s[...] = (dv * lax.rsqrt(var + NORM_EPS) * sgug_ref[...]).astype(BF16)
    causal = (lax.broadcasted_iota(jnp.int32, (CHUNK, CHUNK), 0)
              >= lax.broadcasted_iota(jnp.int32, (CHUNK, CHUNK), 1))
    w_tril = [jnp.where(causal, sguw_ref[g], 0.0).astype(BF16) for g in range(N_HEADS)]

    def sgu_chunk(c, carry):
        r0 = pl.multiple_of(c * CHUNK, CHUNK)
        vc = vn_s[pl.ds(r0, CHUNK), :]
        acc = sgub_ref[...]
        for g in range(N_HEADS):
            acc = acc + _dot(w_tril[g], jnp.where(lane_head == g, vc, jnp.zeros_like(vc)))
        mix_s[pl.ds(r0, CHUNK), :] = acc
        return carry

    lax.fori_loop(0, n_chunks, sgu_chunk, 0)
    y_ref[0, :, 0:gw] = (bu * mix_s[...] * _silu(bg)).astype(BF16)

    pc = _dot(h, win_ref[:, 7 * gw:9 * gw])
    cx = pc[:, 0:gw]
    cg = pc[:, gw:2 * gw]
    hist = POOL_HIST
    e0[hist:hist + tile, :] = cx
    s2[8:hist + tile, :] = e0[8:hist + tile, :] + e0[7:hist + tile - 1, :]
    s4[16:hist + tile, :] = s2[16:hist + tile, :] + s2[14:hist + tile - 2, :]
    s8[24:hist + tile, :] = s4[24:hist + tile, :] + s4[20:hist + tile - 4, :]
    w16 = s8[hist:hist + tile, :] + s8[hist - 8:hist + tile - 8, :]
    lane = lax.broadcasted_iota(jnp.int32, (1, gw), 1)
    win = jnp.where(lane < 64, s2[hist:hist + tile, :],
                    jnp.where(lane < 128, s4[hist:hist + tile, :],
                              jnp.where(lane < 192, s8[hist:hist + tile, :], w16)))
    p_lane = jnp.where(lane < 64, POOL_SIZES[0],
                       jnp.where(lane < 128, POOL_SIZES[1],
                                 jnp.where(lane < 192, POOL_SIZES[2], POOL_SIZES[3])))
    t_glob = n * tile + lax.broadcasted_iota(jnp.int32, (tile, 1), 0)
    cnt = jnp.minimum(t_glob + 1, p_lane).astype(F32)
    pooled = win / cnt - cx
    yc = _dot(pooled.astype(BF16), poolw_ref[...]) * pools_ref[...]
    y_ref[0, :, gw:2 * gw] = (yc * _silu(cg)).astype(BF16)
    e0[0:hist, :] = e0[tile:tile + hist, :]

    pd = _dot(h, win_ref[:, 9 * gw:13 * gw])
    rq_s[...] = pd[:, 0:gw]
    rk_s[...] = pd[:, gw:2 * gw] * (HEAD_DIM ** -0.5)
    rv_s[...] = pd[:, 2 * gw:3 * gw]
    dg = pd[:, 3 * gw:4 * gw]
    row_head = lax.broadcasted_iota(jnp.int32, (gw, 1), 0) // HEAD_DIM
    same_head = row_head == lane_head
    seg = jnp.where(same_head, 1.0 / HEAD_DIM, 0.0).astype(BF16)

    def ret_chunk(c, carry):
        r0 = pl.multiple_of(c * CHUNK, CHUNK)
        qc = rq_s[pl.ds(r0, CHUNK), :]
        kc = rk_s[pl.ds(r0, CHUNK), :]
        vb = rv_s[pl.ds(r0, CHUNK), :].astype(BF16)
        qb = qc.astype(BF16)
        kb = kc.astype(BF16)
        st = state[...]
        of = _dot((qc * xi_ref[...]).astype(BF16), st.astype(BF16))
        for hh in range(N_HEADS):
            in_head = lane_head == hh
            sc = _dot_nt(jnp.where(in_head, qb, jnp.zeros_like(qb)), kb) * decay_ref[hh]
            of = of + _dot(sc.astype(BF16), jnp.where(in_head, vb, jnp.zeros_like(vb)))
        of_s[pl.ds(r0, CHUNK), :] = of
        kv = _dot_tn((kc * zeta_ref[...]).astype(BF16), vb)
        state[...] = st * cdec_ref[...] + jnp.where(same_head, kv, 0.0)
        return carry

    lax.fori_loop(0, n_chunks, ret_chunk, 0)
    of = of_s[...]
    mu_h = _seg_mean(of, seg)
    dh = of - mu_h
    var_h = _seg_mean(dh * dh, seg)
    yd = dh * lax.rsqrt(var_h + NORM_EPS) * retg_ref[...]
    y_ref[0, :, 2 * gw:3 * gw] = (yd * _silu(dg)).astype(BF16)


def _inproj_mix(x, pre_g, w_in, sgu_g, sgu_w, sgu_bias, pool_w_bd, pool_scale, ret_g,
                decay, zeta, xi, cdec):
    b, s, d = x.shape
    tile = IN_TILE
    gw = GROUP_WIDTH
    const2 = lambda bi, ni: (0, 0)
    const3 = lambda bi, ni: (0, 0, 0)
    slab_spec = pl.BlockSpec((1, N_SLABS, tile, LANE), lambda bi, ni: (bi, 0, ni, 0))
    slab_shape = jax.ShapeDtypeStruct((b, N_SLABS, s, LANE), F32)
    return pl.pallas_call(
        _inproj_mix_kernel,
        grid=(b, s // tile),
        in_specs=[
            pl.BlockSpec((1, tile, d), lambda bi, ni: (bi, ni, 0)),
            pl.BlockSpec((1, d), const2),
            pl.BlockSpec(w_in.shape, const2),
            pl.BlockSpec((1, gw), const2),
            pl.BlockSpec(sgu_w.shape, const3),
            pl.BlockSpec((CHUNK, gw), const2),
            pl.BlockSpec((gw, gw), const2),
            pl.BlockSpec((1, gw), const2),
            pl.BlockSpec((1, gw), const2),
            pl.BlockSpec(decay.shape, const3),
            pl.BlockSpec((CHUNK, gw), const2),
            pl.BlockSpec((CHUNK, gw), const2),
            pl.BlockSpec((1, gw), const2),
        ],
        out_specs=[
            slab_spec, slab_spec, slab_spec,
            pl.BlockSpec((1, tile, gw), lambda bi, ni: (bi, ni, 0)),
            pl.BlockSpec((1, tile, 3 * gw), lambda bi, ni: (bi, ni, 0)),
        ],
        out_shape=[
            slab_shape, slab_shape, slab_shape,
            jax.ShapeDtypeStruct((b, s, gw), BF16),
            jax.ShapeDtypeStruct((b, s, 3 * gw), BF16),
        ],
        scratch_shapes=[
            pltpu.VMEM((tile + POOL_HIST, gw), F32),
            pltpu.VMEM((tile + POOL_HIST, gw), F32),
            pltpu.VMEM((tile + POOL_HIST, gw), F32),
            pltpu.VMEM((tile + POOL_HIST, gw), F32),
            pltpu.VMEM((gw, gw), F32),
            pltpu.VMEM((tile, gw), BF16),
            pltpu.VMEM((tile, gw), F32),
            pltpu.VMEM((tile, gw), F32),
            pltpu.VMEM((tile, gw), F32),
            pltpu.VMEM((tile, gw), F32),
            pltpu.VMEM((tile, gw), F32),
        ],
        compiler_params=pltpu.CompilerParams(
            dimension_semantics=("arbitrary", "arbitrary"), vmem_limit_bytes=VMEM_LIMIT),
        name="inproj_mix",
    )(x, pre_g, w_in, sgu_g, sgu_w, sgu_bias, pool_w_bd, pool_scale, ret_g, decay, zeta, xi, cdec)


def _dilated_attn_kernel(q_ref, k_ref, v_ref, ga_ref, o_ref, bias_s, ob_s, lb_s):
    bi = pl.program_id(0)
    qb = pl.program_id(1)
    base = qb * ATT_TILE
    lane_head = _lane_head()

    @pl.when((bi == 0) & (qb == 0))
    def _():
        qi = lax.broadcasted_iota(jnp.int32, (CHUNK, 2 * CHUNK), 0)
        ki = lax.broadcasted_iota(jnp.int32, (CHUNK, 2 * CHUNK), 1)
        dist = qi + CHUNK - ki
        band = (dist >= 0) & (dist <= CHUNK)
        distf = dist.astype(F32)
        for di, dil in enumerate(DILATIONS):
            for hh in range(N_HEADS):
                slope = 2.0 ** (-8.0 * (hh + 1) / N_HEADS)
                bias = -(slope * dil) * distf
                rs = pl.ds(hh * CHUNK, CHUNK)
                bias_s[0, di, rs, :] = jnp.where(band, bias, -jnp.inf)
                bias_s[1, di, rs, :] = jnp.where(band & (ki >= CHUNK), bias, -jnp.inf)

    def rows(ref, start, stride):
        idx = pl.ds(start, CHUNK) if stride == 1 else pl.ds(start, CHUNK, stride=stride)
        return jnp.concatenate([ref[0, s, idx, :] for s in range(N_SLABS)], axis=-1).astype(BF16)

    def attn_block(di, q_start, q_local, blk, first):
        dil = DILATIONS[di]
        span = dil * CHUNK
        cur = q_start
        prev = jnp.where(first, cur, cur - span)
        fi = first.astype(jnp.int32)
        qv = rows(q_ref, q_local, dil)
        kk = jnp.concatenate([rows(k_ref, prev, dil), rows(k_ref, cur, dil)], axis=0)
        vv = jnp.concatenate([rows(v_ref, prev, dil), rows(v_ref, cur, dil)], axis=0)
        qs = jnp.concatenate(
            [jnp.where(lane_head == hh, qv, jnp.zeros_like(qv)) for hh in range(N_HEADS)], axis=0)
        sc = _dot_nt(qs, kk) + bias_s[fi, di]
        m = jnp.max(sc, axis=-1, keepdims=True)
        p = jnp.exp(sc - m)
        l = jnp.sum(p, axis=-1, keepdims=True)
        pv = _dot(p.astype(BF16), vv) / l
        ls = m + jnp.log(l)
        o = pv[0:CHUNK]
        lse = jnp.broadcast_to(ls[0:CHUNK], (CHUNK, GROUP_WIDTH))
        for hh in range(1, N_HEADS):
            in_head = lane_head == hh
            o = jnp.where(in_head, pv[hh * CHUNK:(hh + 1) * CHUNK], o)
            lse = jnp.where(in_head, ls[hh * CHUNK:(hh + 1) * CHUNK], lse)
        return o, lse

    def strided_branch(di, slot):
        dil = DILATIONS[di]
        blocks_per_class = ATT_TILE // (dil * CHUNK)

        def body(i, carry):
            r = i // blocks_per_class
            nb = i % blocks_per_class
            q_local = r + nb * (dil * CHUNK)
            first = (qb == 0) & (nb == 0)
            o, lse = attn_block(di, base + q_local, q_local, nb, first)
            idx = pl.ds(q_local, CHUNK, stride=dil)
            for s in range(N_SLABS):
                ob_s[slot, s, idx, :] = o[:, s * LANE:(s + 1) * LANE]
                lb_s[slot, s, idx, :] = lse[:, s * LANE:(s + 1) * LANE]
            return carry

        lax.fori_loop(0, dil * blocks_per_class, body, 0)

    strided_branch(1, 0)
    strided_branch(2, 1)

    def dense_body(nb, carry):
        q_local = pl.multiple_of(nb * CHUNK, CHUNK)
        first = (qb == 0) & (nb == 0)
        o1, l1 = attn_block(0, base + q_local, q_local, nb, first)
        idx = pl.ds(q_local, CHUNK)
        o4 = jnp.concatenate([ob_s[0, s, idx, :] for s in range(N_SLABS)], axis=-1)
        l4 = jnp.concatenate([lb_s[0, s, idx, :] for s in range(N_SLABS)], axis=-1)
        o16 = jnp.concatenate([ob_s[1, s, idx, :] for s in range(N_SLABS)], axis=-1)
        l16 = jnp.concatenate([lb_s[1, s, idx, :] for s in range(N_SLABS)], axis=-1)
        mx = jnp.maximum(jnp.maximum(l1, l4), l16)
        w1 = jnp.exp(l1 - mx)
        w4 = jnp.exp(l4 - mx)
        w16 = jnp.exp(l16 - mx)
        o = (w1 * o1 + w4 * o4 + w16 * o16) / (w1 + w4 + w16)
        o_ref[0, idx, :] = (o * ga_ref[0, idx, :].astype(F32)).astype(BF16)
        return carry

    lax.fori_loop(0, ATT_TILE // CHUNK, dense_body, 0)


def _dilated_attn(q, k, v, ga):
    b, _, s, _ = q.shape
    gw = GROUP_WIDTH
    n_branch = len(DILATIONS)
    return pl.pallas_call(
        _dilated_attn_kernel,
        grid=(b, s // ATT_TILE),
        in_specs=[
            pl.BlockSpec((1, N_SLABS, ATT_TILE, LANE), lambda bi, qi: (bi, 0, qi, 0)),
            pl.BlockSpec((1, N_SLABS, s, LANE), lambda bi, qi: (bi, 0, 0, 0)),
            pl.BlockSpec((1, N_SLABS, s, LANE), lambda bi, qi: (bi, 0, 0, 0)),
            pl.BlockSpec((1, ATT_TILE, gw), lambda bi, qi: (bi, qi, 0)),
        ],
        out_specs=pl.BlockSpec((1, ATT_TILE, gw), lambda bi, qi: (bi, qi, 0)),
        out_shape=jax.ShapeDtypeStruct((b, s, gw), BF16),
        scratch_shapes=[
            pltpu.VMEM((2, n_branch, N_HEADS * CHUNK, 2 * CHUNK), F32),
            pltpu.VMEM((n_branch - 1, N_SLABS, ATT_TILE, LANE), F32),
            pltpu.VMEM((n_branch - 1, N_SLABS, ATT_TILE, LANE), F32),
        ],
        compiler_params=pltpu.CompilerParams(
            dimension_semantics=("arbitrary", "arbitrary"), vmem_limit_bytes=VMEM_LIMIT),
        name="dilated_attn",
    )(q, k, v, ga)


def _outproj_kernel(x_ref, ya_ref, y_ref, wout_ref, postg_ref, o_ref):
    gw = GROUP_WIDTH
    y = _dot(ya_ref[0], wout_ref[0:gw, :]) + _dot(y_ref[0], wout_ref[gw:, :])
    ms = jnp.mean(y * y, axis=-1, keepdims=True)
    o_ref[0] = x_ref[0] + y * lax.rsqrt(ms + NORM_EPS) * postg_ref[...]


def _outproj(x, ya, ybcd, w_out, post_g):
    b, s, d = x.shape
    tile = IN_TILE
    gw = GROUP_WIDTH
    return pl.pallas_call(
        _outproj_kernel,
        grid=(b, s // tile),
        in_specs=[
            pl.BlockSpec((1, tile, d), lambda bi, ni: (bi, ni, 0)),
            pl.BlockSpec((1, tile, gw), lambda bi, ni: (bi, ni, 0)),
            pl.BlockSpec((1, tile, 3 * gw), lambda bi, ni: (bi, ni, 0)),
            pl.BlockSpec(w_out.shape, lambda bi, ni: (0, 0)),
            pl.BlockSpec((1, d), lambda bi, ni: (0, 0)),
        ],
        out_specs=pl.BlockSpec((1, tile, d), lambda bi, ni: (bi, ni, 0)),
        out_shape=jax.ShapeDtypeStruct(x.shape, x.dtype),
        compiler_params=pltpu.CompilerParams(
            dimension_semantics=("arbitrary", "arbitrary"), vmem_limit_bytes=VMEM_LIMIT),
        name="outproj",
    )(x, ya, ybcd, w_out, post_g)


def _retention_tables():
    log_g = jnp.log(1.0 - jnp.exp2(-5.0 - jnp.arange(N_HEADS, dtype=F32)))
    i = jnp.arange(CHUNK, dtype=F32)
    diff = i[:, None] - i[None, :]
    decay = jnp.where(diff >= 0, jnp.exp(log_g[:, None, None] * jnp.maximum(diff, 0.0)), 0.0)
    zeta = jnp.exp(log_g[:, None] * (CHUNK - 1 - i))
    xi = jnp.exp(log_g[:, None] * (i + 1))
    cdec = jnp.exp(log_g * CHUNK)
    per_lane = lambda t: jnp.repeat(t.T, HEAD_DIM, axis=1)
    return decay, per_lane(zeta), per_lane(xi), jnp.repeat(cdec, HEAD_DIM)[None, :]


def _layer(x, pre_g, w_in, sgu_g, sgu_w, sgu_b, pool_w, pool_scale, ret_g, w_out, post_g, tables):
    decay, zeta, xi, cdec = tables
    sgu_bias = jnp.repeat(sgu_b.T, HEAD_DIM, axis=1)
    pool_w_bd = jax.scipy.linalg.block_diag(*[pool_w[g] for g in range(len(POOL_SIZES))]).astype(BF16)
    q, k, v, ga, ybcd = _inproj_mix(
        x, pre_g[None, :], w_in.astype(BF16), sgu_g[None, :], sgu_w, sgu_bias, pool_w_bd,
        pool_scale[None, :], ret_g[None, :], decay, zeta, xi, cdec)
    ya = _dilated_attn(q, k, v, ga)
    return _outproj(x, ya, ybcd, w_out.astype(BF16), post_g[None, :])


def kernel(x, pre_g, w_in, sgu_g, sgu_w, sgu_b, pool_w, pool_scale, ret_g, w_out, post_g):
    tables = _retention_tables()
    for l in range(pre_g.shape[0]):
        x = _layer(x, pre_g[l], w_in[l], sgu_g[l], sgu_w[l], sgu_b[l], pool_w[l], pool_scale[l],
                   ret_g[l], w_out[l], post_g[l], tables)
    return x
```

```python
import functools
import math

import jax
import jax.numpy as jnp
from jax import lax
from jax.experimental import pallas as pl
from jax.experimental.pallas import tpu as pltpu

F32 = jnp.float32
BF16 = jnp.bfloat16

HEAD_DIM = 64
N_HEADS = 4
GROUP_WIDTH = HEAD_DIM * N_HEADS
LANE = 128
N_SLABS = GROUP_WIDTH // LANE
CHUNK = 128
DILATIONS = (1, 4, 16)
POOL_SIZES = (2, 4, 8, 16)
POOL_HIST = 32
NORM_EPS = 1e-6
IN_TILE = 512
ATT_TILE = CHUNK * DILATIONS[-1]
ATT_UNROLL = 2
VMEM_LIMIT = 56 * 1024 * 1024


def _silu(x):
    return x / (1.0 + jnp.exp(-x))


def _lane_head(width=GROUP_WIDTH):
    return lax.broadcasted_iota(jnp.int32, (1, width), 1) // HEAD_DIM


def _dot(a, b):
    return jnp.dot(a, b, preferred_element_type=F32)


def _dot_nt(a, b):
    return lax.dot_general(a, b, (((1,), (1,)), ((), ())), preferred_element_type=F32)


def _dot_tn(a, b):
    return lax.dot_general(a, b, (((0,), (0,)), ((), ())), preferred_element_type=F32)


def _seg_mean(a, seg):
    hi = a.astype(BF16)
    lo = (a - hi.astype(F32)).astype(BF16)
    return _dot(hi, seg) + _dot(lo, seg)


def _inproj_mix_kernel(x_ref, preg_ref, win_ref, sgug_ref, sguw_ref, sgub_ref, poolw_ref, pools_ref,
                       retg_ref, decay_ref, zeta_ref, xi_ref, cdec_ref,
                       q_ref, k_ref, v_ref, ga_ref, y_ref,
                       e0, s2, s4, s8, state, vn_s, mix_s, rq_s, rk_s, rv_s, of_s):
    n = pl.program_id(1)
    tile = x_ref.shape[1]
    n_chunks = tile // CHUNK
    gw = GROUP_WIDTH

    @pl.when(n == 0)
    def _():
        e0[0:POOL_HIST, :] = jnp.zeros((POOL_HIST, gw), F32)
        state[...] = jnp.zeros_like(state)

    x = x_ref[0]
    ms = jnp.mean(x * x, axis=-1, keepdims=True)
    h = (x * lax.rsqrt(ms + NORM_EPS) * preg_ref[...]).astype(BF16)

    lane_head = _lane_head()

    pa = _dot(h, win_ref[:, 0:4 * gw])
    qa = pa[:, 0:gw] * (HEAD_DIM ** -0.5)
    ka = pa[:, gw:2 * gw]
    va = pa[:, 2 * gw:3 * gw]
    for s in range(N_SLABS):
        q_ref[0, s] = qa[:, s * LANE:(s + 1) * LANE]
        k_ref[0, s] = ka[:, s * LANE:(s + 1) * LANE]
        v_ref[0, s] = va[:, s * LANE:(s + 1) * LANE]
    ga_ref[0] = _silu(pa[:, 3 * gw:4 * gw]).astype(BF16)

    pb = _dot(h, win_ref[:, 4 * gw:7 * gw])
    bu = pb[:, 0:gw]
    bv = pb[:, gw:2 * gw]
    bg = pb[:, 2 * gw:3 * gw]
    mu = jnp.mean(bv, axis=-1, keepdims=True)
    dv = bv - mu
    var = jnp.mean(dv * dv, axis=-1, keepdims=True)
    vn_s[...] = (dv * lax.rsqrt(var + NORM_EPS) * sgug_ref[...]).astype(BF16)
    causal = (lax.broadcasted_iota(jnp.int32, (CHUNK, CHUNK), 0)
              >= lax.broadcasted_iota(jnp.int32, (CHUNK, CHUNK), 1))
    w_cat = jnp.concatenate(
        [jnp.where(causal, sguw_ref[g], 0.0).astype(BF16) for g in range(N_HEADS)], axis=1)
    for c in range(n_chunks):
        vc = vn_s[c * CHUNK:(c + 1) * CHUNK, :]
        v_stack = jnp.concatenate(
            [jnp.where(lane_head == g, vc, jnp.zeros_like(vc)) for g in range(N_HEADS)], axis=0)
        mix_s[c * CHUNK:(c + 1) * CHUNK, :] = sgub_ref[...] + _dot(w_cat, v_stack)
    y_ref[0, :, 0:gw] = (bu * mix_s[...] * _silu(bg)).astype(BF16)

    pc = _dot(h, win_ref[:, 7 * gw:9 * gw])
    cx = pc[:, 0:gw]
    cg = pc[:, gw:2 * gw]
    hist = POOL_HIST
    e0[hist:hist + tile, :] = cx
    s2[8:hist + tile, :] = e0[8:hist + tile, :] + e0[7:hist + tile - 1, :]
    s4[16:hist + tile, :] = s2[16:hist + tile, :] + s2[14:hist + tile - 2, :]
    s8[24:hist + tile, :] = s4[24:hist + tile, :] + s4[20:hist + tile - 4, :]
    w16 = s8[hist:hist + tile, :] + s8[hist - 8:hist + tile - 8, :]
    lane = lax.broadcasted_iota(jnp.int32, (1, gw), 1)
    win = jnp.where(lane < 64, s2[hist:hist + tile, :],
                    jnp.where(lane < 128, s4[hist:hist + tile, :],
                              jnp.where(lane < 192, s8[hist:hist + tile, :], w16)))
    p_lane = jnp.where(lane < 64, POOL_SIZES[0],
                       jnp.where(lane < 128, POOL_SIZES[1],
                                 jnp.where(lane < 192, POOL_SIZES[2], POOL_SIZES[3])))
    t_glob = n * tile + lax.broadcasted_iota(jnp.int32, (tile, 1), 0)
    cnt = jnp.minimum(t_glob + 1, p_lane).astype(F32)
    pooled = win / cnt - cx
    yc = _dot(pooled.astype(BF16), poolw_ref[...]) * pools_ref[...]
    y_ref[0, :, gw:2 * gw] = (yc * _silu(cg)).astype(BF16)
    e0[0:hist, :] = e0[tile:tile + hist, :]

    pd = _dot(h, win_ref[:, 9 * gw:13 * gw])
    rq_s[...] = pd[:, 0:gw]
    rk_s[...] = pd[:, gw:2 * gw] * (HEAD_DIM ** -0.5)
    rv_s[...] = pd[:, 2 * gw:3 * gw]
    dg = pd[:, 3 * gw:4 * gw]
    row_head = lax.broadcasted_iota(jnp.int32, (gw, 1), 0) // HEAD_DIM
    same_head = row_head == lane_head
    seg = jnp.where(same_head, 1.0 / HEAD_DIM, 0.0).astype(BF16)

    st = state[...]
    for c in range(n_chunks):
        rs = slice(c * CHUNK, (c + 1) * CHUNK)
        qc = rq_s[rs, :]
        kc = rk_s[rs, :]
        vb = rv_s[rs, :].astype(BF16)
        kb = kc.astype(BF16)
        k_stack = jnp.concatenate(
            [jnp.where(lane_head == hh, kb, jnp.zeros_like(kb)) for hh in range(N_HEADS)], axis=0)
        v_stack = jnp.concatenate(
            [jnp.where(lane_head == hh, vb, jnp.zeros_like(vb)) for hh in range(N_HEADS)], axis=0)
        sc = _dot_nt(qc.astype(BF16), k_stack) * decay_ref[...]
        of = _dot(sc.astype(BF16), v_stack)
        of_s[rs, :] = of + _dot((qc * xi_ref[...]).astype(BF16), st.astype(BF16))
        kv = _dot_tn((kc * zeta_ref[...]).astype(BF16), vb)
        st = st * cdec_ref[...] + jnp.where(same_head, kv, 0.0)
    state[...] = st
    of = of_s[...]
    mu_h = _seg_mean(of, seg)
    dh = of - mu_h
    var_h = _dot((dh * dh).astype(BF16), seg)
    yd = dh * lax.rsqrt(var_h + NORM_EPS) * retg_ref[...]
    y_ref[0, :, 2 * gw:3 * gw] = (yd * _silu(dg)).astype(BF16)


def _inproj_mix(x, pre_g, w_in, sgu_g, sgu_w, sgu_bias, pool_w_bd, pool_scale, ret_g,
                decay, zeta, xi, cdec):
    b, s, d = x.shape
    tile = IN_TILE
    gw = GROUP_WIDTH
    const2 = lambda bi, ni: (0, 0)
    const3 = lambda bi, ni: (0, 0, 0)
    slab_spec = pl.BlockSpec((1, N_SLABS, tile, LANE), lambda bi, ni: (bi, 0, ni, 0))
    slab_shape = jax.ShapeDtypeStruct((b, N_SLABS, s, LANE), F32)
    return pl.pallas_call(
        _inproj_mix_kernel,
        grid=(b, s // tile),
        in_specs=[
            pl.BlockSpec((1, tile, d), lambda bi, ni: (bi, ni, 0)),
            pl.BlockSpec((1, d), const2),
            pl.BlockSpec(w_in.shape, const2),
            pl.BlockSpec((1, gw), const2),
            pl.BlockSpec(sgu_w.shape, const3),
            pl.BlockSpec((CHUNK, gw), const2),
            pl.BlockSpec((gw, gw), const2),
            pl.BlockSpec((1, gw), const2),
            pl.BlockSpec((1, gw), const2),
            pl.BlockSpec(decay.shape, const2),
            pl.BlockSpec((CHUNK, gw), const2),
            pl.BlockSpec((CHUNK, gw), const2),
            pl.BlockSpec((1, gw), const2),
        ],
        out_specs=[
            slab_spec, slab_spec, slab_spec,
            pl.BlockSpec((1, tile, gw), lambda bi, ni: (bi, ni, 0)),
            pl.BlockSpec((1, tile, 3 * gw), lambda bi, ni: (bi, ni, 0)),
        ],
        out_shape=[
            slab_shape, slab_shape, slab_shape,
            jax.ShapeDtypeStruct((b, s, gw), BF16),
            jax.ShapeDtypeStruct((b, s, 3 * gw), BF16),
        ],
        scratch_shapes=[
            pltpu.VMEM((tile + POOL_HIST, gw), F32),
            pltpu.VMEM((tile + POOL_HIST, gw), F32),
            pltpu.VMEM((tile + POOL_HIST, gw), F32),
            pltpu.VMEM((tile + POOL_HIST, gw), F32),
            pltpu.VMEM((gw, gw), F32),
            pltpu.VMEM((tile, gw), BF16),
            pltpu.VMEM((tile, gw), F32),
            pltpu.VMEM((tile, gw), F32),
            pltpu.VMEM((tile, gw), F32),
            pltpu.VMEM((tile, gw), F32),
            pltpu.VMEM((tile, gw), F32),
        ],
        compiler_params=pltpu.CompilerParams(
            dimension_semantics=("arbitrary", "arbitrary"), vmem_limit_bytes=VMEM_LIMIT),
        name="inproj_mix",
    )(x, pre_g, w_in, sgu_g, sgu_w, sgu_bias, pool_w_bd, pool_scale, ret_g, decay, zeta, xi, cdec)


def _dilated_attn_kernel(q_ref, k_ref, v_ref, ga_ref, o_ref, bias_s, ob_s, lb_s):
    bi = pl.program_id(0)
    qb = pl.program_id(1)
    base = qb * ATT_TILE
    lane_head = _lane_head()

    @pl.when((bi == 0) & (qb == 0))
    def _():
        qi = lax.broadcasted_iota(jnp.int32, (CHUNK, 2 * CHUNK), 0)
        ki = lax.broadcasted_iota(jnp.int32, (CHUNK, 2 * CHUNK), 1)
        dist = qi + CHUNK - ki
        band = (dist >= 0) & (dist <= CHUNK)
        distf = dist.astype(F32)
        for di, dil in enumerate(DILATIONS):
            for hh in range(N_HEADS):
                slope = 2.0 ** (-8.0 * (hh + 1) / N_HEADS)
                bias = -(slope * dil) * distf
                rs = pl.ds(hh * CHUNK, CHUNK)
                bias_s[0, di, rs, :] = jnp.where(band, bias, -jnp.inf)
                bias_s[1, di, rs, :] = jnp.where(band & (ki >= CHUNK), bias, -jnp.inf)

    def rows(ref, start, stride):
        idx = pl.ds(start, CHUNK) if stride == 1 else pl.ds(start, CHUNK, stride=stride)
        return jnp.concatenate([ref[0, s, idx, :] for s in range(N_SLABS)], axis=-1).astype(BF16)

    def attn_block(di, q_start, q_local, blk, first):
        dil = DILATIONS[di]
        span = dil * CHUNK
        cur = q_start
        prev = jnp.where(first, cur, cur - span)
        fi = first.astype(jnp.int32)
        qv = rows(q_ref, q_local, dil)
        kk = jnp.concatenate([rows(k_ref, prev, dil), rows(k_ref, cur, dil)], axis=0)
        vv = jnp.concatenate([rows(v_ref, prev, dil), rows(v_ref, cur, dil)], axis=0)
        qs = jnp.concatenate(
            [jnp.where(lane_head == hh, qv, jnp.zeros_like(qv)) for hh in range(N_HEADS)], axis=0)
        sc = _dot_nt(qs, kk) + bias_s[fi, di]
        m = jnp.max(sc, axis=-1, keepdims=True)
        p = jnp.exp(sc - m)
        l = jnp.sum(p, axis=-1, keepdims=True)
        pv = _dot(p.astype(BF16), vv) / l
        ls = m + jnp.log(l)
        o = pv[0:CHUNK]
        lse = jnp.broadcast_to(ls[0:CHUNK], (CHUNK, GROUP_WIDTH))
        for hh in range(1, N_HEADS):
            in_head = lane_head == hh
            o = jnp.where(in_head, pv[hh * CHUNK:(hh + 1) * CHUNK], o)
            lse = jnp.where(in_head, ls[hh * CHUNK:(hh + 1) * CHUNK], lse)
        return o, lse

    def strided_branch(di, slot):
        dil = DILATIONS[di]
        blocks_per_class = ATT_TILE // (dil * CHUNK)

        def body(i, carry):
            r = i // blocks_per_class
            nb = i % blocks_per_class
            q_local = r + nb * (dil * CHUNK)
            first = (qb == 0) & (nb == 0)
            o, lse = attn_block(di, base + q_local, q_local, nb, first)
            idx = pl.ds(q_local, CHUNK, stride=dil)
            for s in range(N_SLABS):
                ob_s[slot, s, idx, :] = o[:, s * LANE:(s + 1) * LANE]
                lb_s[slot, s, idx, :] = lse[:, s * LANE:(s + 1) * LANE]
            return carry

        lax.fori_loop(0, dil * blocks_per_class, body, 0, unroll=ATT_UNROLL)

    strided_branch(1, 0)
    strided_branch(2, 1)

    def dense_body(nb, carry):
        q_local = pl.multiple_of(nb * CHUNK, CHUNK)
        first = (qb == 0) & (nb == 0)
        o1, l1 = attn_block(0, base + q_local, q_local, nb, first)
        idx = pl.ds(q_local, CHUNK)
        o4 = jnp.concatenate([ob_s[0, s, idx, :] for s in range(N_SLABS)], axis=-1)
        l4 = jnp.concatenate([lb_s[0, s, idx, :] for s in range(N_SLABS)], axis=-1)
        o16 = jnp.concatenate([ob_s[1, s, idx, :] for s in range(N_SLABS)], axis=-1)
        l16 = jnp.concatenate([lb_s[1, s, idx, :] for s in range(N_SLABS)], axis=-1)
        mx = jnp.maximum(jnp.maximum(l1, l4), l16)
        w1 = jnp.exp(l1 - mx)
        w4 = jnp.exp(l4 - mx)
        w16 = jnp.exp(l16 - mx)
        o = (w1 * o1 + w4 * o4 + w16 * o16) / (w1 + w4 + w16)
        o_ref[0, idx, :] = (o * ga_ref[0, idx, :].astype(F32)).astype(BF16)
        return carry

    lax.fori_loop(0, ATT_TILE // CHUNK, dense_body, 0, unroll=ATT_UNROLL)


def _dilated_attn(q, k, v, ga):
    b, _, s, _ = q.shape
    gw = GROUP_WIDTH
    n_branch = len(DILATIONS)
    return pl.pallas_call(
        _dilated_attn_kernel,
        grid=(b, s // ATT_TILE),
        in_specs=[
            pl.BlockSpec((1, N_SLABS, ATT_TILE, LANE), lambda bi, qi: (bi, 0, qi, 0)),
            pl.BlockSpec((1, N_SLABS, s, LANE), lambda bi, qi: (bi, 0, 0, 0)),
            pl.BlockSpec((1, N_SLABS, s, LANE), lambda bi, qi: (bi, 0, 0, 0)),
            pl.BlockSpec((1, ATT_TILE, gw), lambda bi, qi: (bi, qi, 0)),
        ],
        out_specs=pl.BlockSpec((1, ATT_TILE, gw), lambda bi, qi: (bi, qi, 0)),
        out_shape=jax.ShapeDtypeStruct((b, s, gw), BF16),
        scratch_shapes=[
            pltpu.VMEM((2, n_branch, N_HEADS * CHUNK, 2 * CHUNK), F32),
            pltpu.VMEM((n_branch - 1, N_SLABS, ATT_TILE, LANE), F32),
            pltpu.VMEM((n_branch - 1, N_SLABS, ATT_TILE, LANE), F32),
        ],
        compiler_params=pltpu.CompilerParams(
            dimension_semantics=("arbitrary", "arbitrary"), vmem_limit_bytes=VMEM_LIMIT),
        name="dilated_attn",
    )(q, k, v, ga)


def _outproj_kernel(x_ref, ya_ref, y_ref, wout_ref, postg_ref, o_ref):
    gw = GROUP_WIDTH
    y = _dot(ya_ref[0], wout_ref[0:gw, :]) + _dot(y_ref[0], wout_ref[gw:, :])
    ms = jnp.mean(y * y, axis=-1, keepdims=True)
    o_ref[0] = x_ref[0] + y * lax.rsqrt(ms + NORM_EPS) * postg_ref[...]


def _outproj(x, ya, ybcd, w_out, post_g):
    b, s, d = x.shape
    tile = IN_TILE
    gw = GROUP_WIDTH
    return pl.pallas_call(
        _outproj_kernel,
        grid=(b, s // tile),
        in_specs=[
            pl.BlockSpec((1, tile, d), lambda bi, ni: (bi, ni, 0)),
            pl.BlockSpec((1, tile, gw), lambda bi, ni: (bi, ni, 0)),
            pl.BlockSpec((1, tile, 3 * gw), lambda bi, ni: (bi, ni, 0)),
            pl.BlockSpec(w_out.shape, lambda bi, ni: (0, 0)),
            pl.BlockSpec((1, d), lambda bi, ni: (0, 0)),
        ],
        out_specs=pl.BlockSpec((1, tile, d), lambda bi, ni: (bi, ni, 0)),
        out_shape=jax.ShapeDtypeStruct(x.shape, x.dtype),
        compiler_params=pltpu.CompilerParams(
            dimension_semantics=("arbitrary", "arbitrary"), vmem_limit_bytes=VMEM_LIMIT),
        name="outproj",
    )(x, ya, ybcd, w_out, post_g)


def _retention_tables():
    log_g = jnp.log(1.0 - jnp.exp2(-5.0 - jnp.arange(N_HEADS, dtype=F32)))
    i = jnp.arange(CHUNK, dtype=F32)
    diff = i[:, None] - i[None, :]
    decay = jnp.where(diff >= 0, jnp.exp(log_g[:, None, None] * jnp.maximum(diff, 0.0)), 0.0)
    zeta = jnp.exp(log_g[:, None] * (CHUNK - 1 - i))
    xi = jnp.exp(log_g[:, None] * (i + 1))
    cdec = jnp.exp(log_g * CHUNK)
    per_lane = lambda t: jnp.repeat(t.T, HEAD_DIM, axis=1)
    decay = decay.transpose(1, 0, 2).reshape(CHUNK, N_HEADS * CHUNK)
    return decay, per_lane(zeta), per_lane(xi), jnp.repeat(cdec, HEAD_DIM)[None, :]


def _layer(x, pre_g, w_in, sgu_g, sgu_w, sgu_b, pool_w, pool_scale, ret_g, w_out, post_g, tables):
    decay, zeta, xi, cdec = tables
    sgu_bias = jnp.repeat(sgu_b.T, HEAD_DIM, axis=1)
    pool_w_bd = jax.scipy.linalg.block_diag(*[pool_w[g] for g in range(len(POOL_SIZES))]).astype(BF16)
    q, k, v, ga, ybcd = _inproj_mix(
        x, pre_g[None, :], w_in.astype(BF16), sgu_g[None, :], sgu_w, sgu_bias, pool_w_bd,
        pool_scale[None, :], ret_g[None, :], decay, zeta, xi, cdec)
    ya = _dilated_attn(q, k, v, ga)
    return _outproj(x, ya, ybcd, w_out.astype(BF16), post_g[None, :])


def kernel(x, pre_g, w_in, sgu_g, sgu_w, sgu_b, pool_w, pool_scale, ret_g, w_out, post_g):
    tables = _retention_tables()
    for l in range(pre_g.shape[0]):
        x = _layer(x, pre_g[l], w_in[l], sgu_g[l], sgu_w[l], sgu_b[l], pool_w[l], pool_scale[l],
                   ret_g[l], w_out[l], post_g[l], tables)
    return x
```

```python
import functools
import math

import jax
import jax.numpy as jnp
from jax import lax
from jax.experimental import pallas as pl
from jax.experimental.pallas import tpu as pltpu

F32 = jnp.float32
BF16 = jnp.bfloat16

HEAD_DIM = 64
N_HEADS = 4
GROUP_WIDTH = HEAD_DIM * N_HEADS
LANE = 128
N_SLABS = GROUP_WIDTH // LANE
CHUNK = 128
DILATIONS = (1, 4, 16)
POOL_SIZES = (2, 4, 8, 16)
POOL_HIST = 32
NORM_EPS = 1e-6
LOG2E = math.log2(math.e)
IN_TILE = 512
ATT_TILE = CHUNK * DILATIONS[-1]
ATT_UNROLL = 4
VMEM_LIMIT = 56 * 1024 * 1024


def _silu(x):
    return x / (1.0 + jnp.exp(-x))


def _lane_head(width=GROUP_WIDTH):
    return lax.broadcasted_iota(jnp.int32, (1, width), 1) // HEAD_DIM


def _dot(a, b):
    return jnp.dot(a, b, preferred_element_type=F32)


def _dot_nt(a, b):
    return lax.dot_general(a, b, (((1,), (1,)), ((), ())), preferred_element_type=F32)


def _dot_tn(a, b):
    return lax.dot_general(a, b, (((0,), (0,)), ((), ())), preferred_element_type=F32)


def _seg_mean(a, seg):
    hi = a.astype(BF16)
    lo = (a - hi.astype(F32)).astype(BF16)
    return _dot(hi, seg) + _dot(lo, seg)


def _inproj_mix_kernel(x_ref, preg_ref, win_ref, sgug_ref, sguw_ref, sgub_ref, poolw_ref, pools_ref,
                       retg_ref, decay_ref, zeta_ref, xi_ref, cdec_ref,
                       q_ref, k_ref, v_ref, ga_ref, y_ref,
                       e0, s2, s4, s8, state, vn_s, mix_s, rq_s, rk_s, rv_s, of_s):
    n = pl.program_id(1)
    tile = x_ref.shape[1]
    n_chunks = tile // CHUNK
    gw = GROUP_WIDTH

    @pl.when(n == 0)
    def _():
        e0[0:POOL_HIST, :] = jnp.zeros((POOL_HIST, gw), F32)
        state[...] = jnp.zeros_like(state)

    x = x_ref[0]
    ms = jnp.mean(x * x, axis=-1, keepdims=True)
    h = (x * lax.rsqrt(ms + NORM_EPS) * preg_ref[...]).astype(BF16)

    lane_head = _lane_head()

    pa = _dot(h, win_ref[:, 0:4 * gw])
    qa = pa[:, 0:gw] * (HEAD_DIM ** -0.5 * LOG2E)
    ka = pa[:, gw:2 * gw]
    va = pa[:, 2 * gw:3 * gw]
    for s in range(N_SLABS):
        q_ref[0, s] = qa[:, s * LANE:(s + 1) * LANE]
        k_ref[0, s] = ka[:, s * LANE:(s + 1) * LANE]
        v_ref[0, s] = va[:, s * LANE:(s + 1) * LANE]
    ga_ref[0] = _silu(pa[:, 3 * gw:4 * gw]).astype(BF16)

    pb = _dot(h, win_ref[:, 4 * gw:7 * gw])
    bu = pb[:, 0:gw]
    bv = pb[:, gw:2 * gw]
    bg = pb[:, 2 * gw:3 * gw]
    mu = jnp.mean(bv, axis=-1, keepdims=True)
    dv = bv - mu
    var = jnp.mean(dv * dv, axis=-1, keepdims=True)
    vn_s[...] = (dv * lax.rsqrt(var + NORM_EPS) * sgug_ref[...]).astype(BF16)
    causal = (lax.broadcasted_iota(jnp.int32, (CHUNK, CHUNK), 0)
              >= lax.broadcasted_iota(jnp.int32, (CHUNK, CHUNK), 1))
    w_cat = jnp.concatenate(
        [jnp.where(causal, sguw_ref[g], 0.0).astype(BF16) for g in range(N_HEADS)], axis=1)
    for c in range(n_chunks):
        vc = vn_s[c * CHUNK:(c + 1) * CHUNK, :]
        v_stack = jnp.concatenate(
            [jnp.where(lane_head == g, vc, jnp.zeros_like(vc)) for g in range(N_HEADS)], axis=0)
        mix_s[c * CHUNK:(c + 1) * CHUNK, :] = sgub_ref[...] + _dot(w_cat, v_stack)
    y_ref[0, :, 0:gw] = (bu * mix_s[...] * _silu(bg)).astype(BF16)

    pc = _dot(h, win_ref[:, 7 * gw:9 * gw])
    cx = pc[:, 0:gw]
    cg = pc[:, gw:2 * gw]
    hist = POOL_HIST
    e0[hist:hist + tile, :] = cx
    s2[8:hist + tile, :] = e0[8:hist + tile, :] + e0[7:hist + tile - 1, :]
    s4[16:hist + tile, :] = s2[16:hist + tile, :] + s2[14:hist + tile - 2, :]
    s8[24:hist + tile, :] = s4[24:hist + tile, :] + s4[20:hist + tile - 4, :]
    w16 = s8[hist:hist + tile, :] + s8[hist - 8:hist + tile - 8, :]
    lane = lax.broadcasted_iota(jnp.int32, (1, gw), 1)
    win = jnp.where(lane < 64, s2[hist:hist + tile, :],
                    jnp.where(lane < 128, s4[hist:hist + tile, :],
                              jnp.where(lane < 192, s8[hist:hist + tile, :], w16)))
    p_lane = jnp.where(lane < 64, POOL_SIZES[0],
                       jnp.where(lane < 128, POOL_SIZES[1],
                                 jnp.where(lane < 192, POOL_SIZES[2], POOL_SIZES[3])))
    t_glob = n * tile + lax.broadcasted_iota(jnp.int32, (tile, 1), 0)
    cnt = jnp.minimum(t_glob + 1, p_lane).astype(F32)
    pooled = win / cnt - cx
    yc = _dot(pooled.astype(BF16), poolw_ref[...]) * pools_ref[...]
    y_ref[0, :, gw:2 * gw] = (yc * _silu(cg)).astype(BF16)
    e0[0:hist, :] = e0[tile:tile + hist, :]

    pd = _dot(h, win_ref[:, 9 * gw:13 * gw])
    rq_s[...] = pd[:, 0:gw]
    rk_s[...] = pd[:, gw:2 * gw] * (HEAD_DIM ** -0.5)
    rv_s[...] = pd[:, 2 * gw:3 * gw]
    dg = pd[:, 3 * gw:4 * gw]
    row_head = lax.broadcasted_iota(jnp.int32, (gw, 1), 0) // HEAD_DIM
    same_head = row_head == lane_head
    seg = jnp.where(same_head, 1.0 / HEAD_DIM, 0.0).astype(BF16)

    st = state[...]
    for c in range(n_chunks):
        rs = slice(c * CHUNK, (c + 1) * CHUNK)
        qc = rq_s[rs, :]
        kc = rk_s[rs, :]
        vb = rv_s[rs, :].astype(BF16)
        kb = kc.astype(BF16)
        k_stack = jnp.concatenate(
            [jnp.where(lane_head == hh, kb, jnp.zeros_like(kb)) for hh in range(N_HEADS)], axis=0)
        v_stack = jnp.concatenate(
            [jnp.where(lane_head == hh, vb, jnp.zeros_like(vb)) for hh in range(N_HEADS)], axis=0)
        sc = _dot_nt(qc.astype(BF16), k_stack) * decay_ref[...]
        of = _dot(sc.astype(BF16), v_stack)
        of_s[rs, :] = of + _dot((qc * xi_ref[...]).astype(BF16), st.astype(BF16))
        kv = _dot_tn((kc * zeta_ref[...]).astype(BF16), vb)
        st = st * cdec_ref[...] + jnp.where(same_head, kv, 0.0)
    state[...] = st
    of = of_s[...]
    mu_h = _seg_mean(of, seg)
    dh = of - mu_h
    var_h = _dot((dh * dh).astype(BF16), seg)
    yd = dh * lax.rsqrt(var_h + NORM_EPS) * retg_ref[...]
    y_ref[0, :, 2 * gw:3 * gw] = (yd * _silu(dg)).astype(BF16)


def _inproj_mix(x, pre_g, w_in, sgu_g, sgu_w, sgu_bias, pool_w_bd, pool_scale, ret_g,
                decay, zeta, xi, cdec):
    b, s, d = x.shape
    tile = IN_TILE
    gw = GROUP_WIDTH
    const2 = lambda bi, ni: (0, 0)
    const3 = lambda bi, ni: (0, 0, 0)
    slab_spec = pl.BlockSpec((1, N_SLABS, tile, LANE), lambda bi, ni: (bi, 0, ni, 0))
    slab_shape = jax.ShapeDtypeStruct((b, N_SLABS, s, LANE), F32)
    return pl.pallas_call(
        _inproj_mix_kernel,
        grid=(b, s // tile),
        in_specs=[
            pl.BlockSpec((1, tile, d), lambda bi, ni: (bi, ni, 0)),
            pl.BlockSpec((1, d), const2),
            pl.BlockSpec(w_in.shape, const2),
            pl.BlockSpec((1, gw), const2),
            pl.BlockSpec(sgu_w.shape, const3),
            pl.BlockSpec((CHUNK, gw), const2),
            pl.BlockSpec((gw, gw), const2),
            pl.BlockSpec((1, gw), const2),
            pl.BlockSpec((1, gw), const2),
            pl.BlockSpec(decay.shape, const2),
            pl.BlockSpec((CHUNK, gw), const2),
            pl.BlockSpec((CHUNK, gw), const2),
            pl.BlockSpec((1, gw), const2),
        ],
        out_specs=[
            slab_spec, slab_spec, slab_spec,
            pl.BlockSpec((1, tile, gw), lambda bi, ni: (bi, ni, 0)),
            pl.BlockSpec((1, tile, 3 * gw), lambda bi, ni: (bi, ni, 0)),
        ],
        out_shape=[
            slab_shape, slab_shape, slab_shape,
            jax.ShapeDtypeStruct((b, s, gw), BF16),
            jax.ShapeDtypeStruct((b, s, 3 * gw), BF16),
        ],
        scratch_shapes=[
            pltpu.VMEM((tile + POOL_HIST, gw), F32),
            pltpu.VMEM((tile + POOL_HIST, gw), F32),
            pltpu.VMEM((tile + POOL_HIST, gw), F32),
            pltpu.VMEM((tile + POOL_HIST, gw), F32),
            pltpu.VMEM((gw, gw), F32),
            pltpu.VMEM((tile, gw), BF16),
            pltpu.VMEM((tile, gw), F32),
            pltpu.VMEM((tile, gw), F32),
            pltpu.VMEM((tile, gw), F32),
            pltpu.VMEM((tile, gw), F32),
            pltpu.VMEM((tile, gw), F32),
        ],
        compiler_params=pltpu.CompilerParams(
            dimension_semantics=("arbitrary", "arbitrary"), vmem_limit_bytes=VMEM_LIMIT),
        name="inproj_mix",
    )(x, pre_g, w_in, sgu_g, sgu_w, sgu_bias, pool_w_bd, pool_scale, ret_g, decay, zeta, xi, cdec)


def _dilated_attn_kernel(q_ref, k_ref, v_ref, ga_ref, o_ref, bias_s, ob_s, lb_s):
    bi = pl.program_id(0)
    qb = pl.program_id(1)
    base = qb * ATT_TILE
    lane_head = _lane_head()

    @pl.when((bi == 0) & (qb == 0))
    def _():
        ki = lax.broadcasted_iota(jnp.int32, (2 * CHUNK, CHUNK), 0)
        qi = lax.broadcasted_iota(jnp.int32, (2 * CHUNK, CHUNK), 1)
        dist = qi + CHUNK - ki
        band = (dist >= 0) & (dist <= CHUNK)
        distf = dist.astype(F32)
        for di, dil in enumerate(DILATIONS):
            for hh in range(N_HEADS):
                slope = 2.0 ** (-8.0 * (hh + 1) / N_HEADS)
                bias = -(slope * dil * LOG2E) * distf
                cs = pl.ds(hh * CHUNK, CHUNK)
                bias_s[0, di, :, cs] = jnp.where(band, bias, -jnp.inf)
                bias_s[1, di, :, cs] = jnp.where(band & (ki >= CHUNK), bias, -jnp.inf)

    def rows(ref, start, stride, dtype=BF16):
        idx = pl.ds(start, CHUNK) if stride == 1 else pl.ds(start, CHUNK, stride=stride)
        return jnp.concatenate([ref[0, s, idx, :] for s in range(N_SLABS)], axis=-1).astype(dtype)

    def attn_scores(di, q_start, q_local, first):
        dil = DILATIONS[di]
        span = dil * CHUNK
        cur = q_start
        prev = jnp.where(first, cur, cur - span)
        fi = first.astype(jnp.int32)
        qv = rows(q_ref, q_local, dil)
        kk = jnp.concatenate([rows(k_ref, prev, dil), rows(k_ref, cur, dil)], axis=0)
        vv = jnp.concatenate([rows(v_ref, prev, dil, F32), rows(v_ref, cur, dil, F32)], axis=0)
        v_t = vv.T.astype(BF16)
        qs = jnp.concatenate(
            [jnp.where(lane_head == hh, qv, jnp.zeros_like(qv)) for hh in range(N_HEADS)], axis=0)
        sc = _dot_nt(kk, qs) + bias_s[fi, di]
        return sc, v_t

    def attn_values(sc, v_t):
        m = jnp.max(sc, axis=0, keepdims=True)
        p = jnp.exp2(sc - m)
        l = jnp.sum(p, axis=0, keepdims=True)
        pb = p.astype(BF16)
        inv = 1.0 / l
        ls = m + jnp.log2(l)
        o_parts, lse_parts = [], []
        for hh in range(N_HEADS):
            cs = slice(hh * CHUNK, (hh + 1) * CHUNK)
            oh = _dot(v_t[hh * HEAD_DIM:(hh + 1) * HEAD_DIM, :], pb[:, cs])
            o_parts.append(oh * inv[:, cs])
            lse_parts.append(jnp.broadcast_to(ls[:, cs], (HEAD_DIM, CHUNK)))
        o_t = jnp.concatenate(o_parts, axis=0)
        lse_t = jnp.concatenate(lse_parts, axis=0)
        return o_t.T, lse_t.T

    def strided_branch(di, slot):
        dil = DILATIONS[di]
        blocks_per_class = ATT_TILE // (dil * CHUNK)

        def body(it, carry):
            staged = []
            for u in range(ATT_UNROLL):
                i = it * ATT_UNROLL + u
                r = i // blocks_per_class
                nb = i % blocks_per_class
                q_local = r + nb * (dil * CHUNK)
                first = (qb == 0) & (nb == 0)
                staged.append((q_local, attn_scores(di, base + q_local, q_local, first)))
            for q_local, (sc, v_t) in staged:
                o, lse = attn_values(sc, v_t)
                idx = pl.ds(q_local, CHUNK, stride=dil)
                for s in range(N_SLABS):
                    ob_s[slot, s, idx, :] = o[:, s * LANE:(s + 1) * LANE]
                    lb_s[slot, s, idx, :] = lse[:, s * LANE:(s + 1) * LANE]
            return carry

        lax.fori_loop(0, dil * blocks_per_class // ATT_UNROLL, body, 0)

    strided_branch(1, 0)
    strided_branch(2, 1)

    def dense_body(it, carry):
        staged = []
        for u in range(ATT_UNROLL):
            nb = it * ATT_UNROLL + u
            q_local = pl.multiple_of(nb * CHUNK, CHUNK)
            first = (qb == 0) & (nb == 0)
            staged.append((q_local, attn_scores(0, base + q_local, q_local, first)))
        for q_local, (sc, v_t) in staged:
            o1, l1 = attn_values(sc, v_t)
            idx = pl.ds(q_local, CHUNK)
            o4 = jnp.concatenate([ob_s[0, s, idx, :] for s in range(N_SLABS)], axis=-1)
            l4 = jnp.concatenate([lb_s[0, s, idx, :] for s in range(N_SLABS)], axis=-1)
            o16 = jnp.concatenate([ob_s[1, s, idx, :] for s in range(N_SLABS)], axis=-1)
            l16 = jnp.concatenate([lb_s[1, s, idx, :] for s in range(N_SLABS)], axis=-1)
            mx = jnp.maximum(jnp.maximum(l1, l4), l16)
            w1 = jnp.exp2(l1 - mx)
            w4 = jnp.exp2(l4 - mx)
            w16 = jnp.exp2(l16 - mx)
            o = (w1 * o1 + w4 * o4 + w16 * o16) / (w1 + w4 + w16)
            o_ref[0, idx, :] = (o * ga_ref[0, idx, :].astype(F32)).astype(BF16)
        return carry

    lax.fori_loop(0, ATT_TILE // CHUNK // ATT_UNROLL, dense_body, 0)


def _dilated_attn(q, k, v, ga):
    b, _, s, _ = q.shape
    gw = GROUP_WIDTH
    n_branch = len(DILATIONS)
    return pl.pallas_call(
        _dilated_attn_kernel,
        grid=(b, s // ATT_TILE),
        in_specs=[
            pl.BlockSpec((1, N_SLABS, ATT_TILE, LANE), lambda bi, qi: (bi, 0, qi, 0)),
            pl.BlockSpec((1, N_SLABS, s, LANE), lambda bi, qi: (bi, 0, 0, 0)),
            pl.BlockSpec((1, N_SLABS, s, LANE), lambda bi, qi: (bi, 0, 0, 0)),
            pl.BlockSpec((1, ATT_TILE, gw), lambda bi, qi: (bi, qi, 0)),
        ],
        out_specs=pl.BlockSpec((1, ATT_TILE, gw), lambda bi, qi: (bi, qi, 0)),
        out_shape=jax.ShapeDtypeStruct((b, s, gw), BF16),
        scratch_shapes=[
            pltpu.VMEM((2, n_branch, 2 * CHUNK, N_HEADS * CHUNK), F32),
            pltpu.VMEM((n_branch - 1, N_SLABS, ATT_TILE, LANE), F32),
            pltpu.VMEM((n_branch - 1, N_SLABS, ATT_TILE, LANE), F32),
        ],
        compiler_params=pltpu.CompilerParams(
            dimension_semantics=("arbitrary", "arbitrary"), vmem_limit_bytes=VMEM_LIMIT),
        name="dilated_attn",
    )(q, k, v, ga)


def _outproj_kernel(x_ref, ya_ref, y_ref, wout_ref, postg_ref, o_ref):
    gw = GROUP_WIDTH
    y = _dot(ya_ref[0], wout_ref[0:gw, :]) + _dot(y_ref[0], wout_ref[gw:, :])
    ms = jnp.mean(y * y, axis=-1, keepdims=True)
    o_ref[0] = x_ref[0] + y * lax.rsqrt(ms + NORM_EPS) * postg_ref[...]


def _outproj(x, ya, ybcd, w_out, post_g):
    b, s, d = x.shape
    tile = IN_TILE
    gw = GROUP_WIDTH
    return pl.pallas_call(
        _outproj_kernel,
        grid=(b, s // tile),
        in_specs=[
            pl.BlockSpec((1, tile, d), lambda bi, ni: (bi, ni, 0)),
            pl.BlockSpec((1, tile, gw), lambda bi, ni: (bi, ni, 0)),
            pl.BlockSpec((1, tile, 3 * gw), lambda bi, ni: (bi, ni, 0)),
            pl.BlockSpec(w_out.shape, lambda bi, ni: (0, 0)),
            pl.BlockSpec((1, d), lambda bi, ni: (0, 0)),
        ],
        out_specs=pl.BlockSpec((1, tile, d), lambda bi, ni: (bi, ni, 0)),
        out_shape=jax.ShapeDtypeStruct(x.shape, x.dtype),
        compiler_params=pltpu.CompilerParams(
            dimension_semantics=("arbitrary", "arbitrary"), vmem_limit_bytes=VMEM_LIMIT),
        name="outproj",
    )(x, ya, ybcd, w_out, post_g)


def _retention_tables():
    log_g = jnp.log(1.0 - jnp.exp2(-5.0 - jnp.arange(N_HEADS, dtype=F32)))
    i = jnp.arange(CHUNK, dtype=F32)
    diff = i[:, None] - i[None, :]
    decay = jnp.where(diff >= 0, jnp.exp(log_g[:, None, None] * jnp.maximum(diff, 0.0)), 0.0)
    zeta = jnp.exp(log_g[:, None] * (CHUNK - 1 - i))
    xi = jnp.exp(log_g[:, None] * (i + 1))
    cdec = jnp.exp(log_g * CHUNK)
    per_lane = lambda t: jnp.repeat(t.T, HEAD_DIM, axis=1)
    decay = decay.transpose(1, 0, 2).reshape(CHUNK, N_HEADS * CHUNK)
    return decay, per_lane(zeta), per_lane(xi), jnp.repeat(cdec, HEAD_DIM)[None, :]


def _layer(x, pre_g, w_in, sgu_g, sgu_w, sgu_b, pool_w, pool_scale, ret_g, w_out, post_g, tables):
    decay, zeta, xi, cdec = tables
    sgu_bias = jnp.repeat(sgu_b.T, HEAD_DIM, axis=1)
    pool_w_bd = jax.scipy.linalg.block_diag(*[pool_w[g] for g in range(len(POOL_SIZES))]).astype(BF16)
    q, k, v, ga, ybcd = _inproj_mix(
        x, pre_g[None, :], w_in.astype(BF16), sgu_g[None, :], sgu_w, sgu_bias, pool_w_bd,
        pool_scale[None, :], ret_g[None, :], decay, zeta, xi, cdec)
    ya = _dilated_attn(q, k, v, ga)
    return _outproj(x, ya, ybcd, w_out.astype(BF16), post_g[None, :])


def kernel(x, pre_g, w_in, sgu_g, sgu_w, sgu_b, pool_w, pool_scale, ret_g, w_out, post_g):
    tables = _retention_tables()
    for l in range(pre_g.shape[0]):
        x = _layer(x, pre_g[l], w_in[l], sgu_g[l], sgu_w[l], sgu_b[l], pool_w[l], pool_scale[l],
                   ret_g[l], w_out[l], post_g[l], tables)
    return x
```

```python
import functools
import math

import jax
import jax.numpy as jnp
from jax import lax
from jax.experimental import pallas as pl
from jax.experimental.pallas import tpu as pltpu

F32 = jnp.float32
BF16 = jnp.bfloat16

HEAD_DIM = 64
N_HEADS = 4
GROUP_WIDTH = HEAD_DIM * N_HEADS
LANE = 128
N_SLABS = GROUP_WIDTH // LANE
CHUNK = 128
DILATIONS = (1, 4, 16)
POOL_SIZES = (2, 4, 8, 16)
POOL_HIST = 32
NORM_EPS = 1e-6
LOG2E = math.log2(math.e)
IN_TILE = 512
OUT_TILE = 1024
ATT_TILE = CHUNK * DILATIONS[-1]
ATT_GROUP = 8
ATT_LOOKAHEAD = 1
VMEM_LIMIT = 56 * 1024 * 1024


def _silu(x):
    return x / (1.0 + jnp.exp(-x))


def _lane_head(width=GROUP_WIDTH):
    return lax.broadcasted_iota(jnp.int32, (1, width), 1) // HEAD_DIM


def _dot(a, b):
    return jnp.dot(a, b, preferred_element_type=F32)


def _dot_nt(a, b):
    return lax.dot_general(a, b, (((1,), (1,)), ((), ())), preferred_element_type=F32)


def _dot_tn(a, b):
    return lax.dot_general(a, b, (((0,), (0,)), ((), ())), preferred_element_type=F32)


def _seg_mean(a, seg):
    hi = a.astype(BF16)
    lo = (a - hi.astype(F32)).astype(BF16)
    return _dot(hi, seg) + _dot(lo, seg)


def _inproj_mix_kernel(x_ref, preg_ref, win_ref, sgug_ref, sguw_ref, sgub_ref, poolw_ref, pools_ref,
                       retg_ref, decay_ref, zeta_ref, xi_ref, cdec_ref,
                       q_ref, k_ref, v_ref, ga_ref, y_ref,
                       e0, s2, s4, s8, state, vn_s, mix_s, rq_s, rk_s, rv_s, of_s):
    n = pl.program_id(1)
    tile = x_ref.shape[1]
    n_chunks = tile // CHUNK
    gw = GROUP_WIDTH

    @pl.when(n == 0)
    def _():
        e0[0:POOL_HIST, :] = jnp.zeros((POOL_HIST, gw), F32)
        state[...] = jnp.zeros_like(state)

    x = x_ref[0]
    ms = jnp.mean(x * x, axis=-1, keepdims=True)
    h = (x * lax.rsqrt(ms + NORM_EPS) * preg_ref[...]).astype(BF16)

    lane_head = _lane_head()

    pb = _dot(h, win_ref[:, 4 * gw:7 * gw])
    pc = _dot(h, win_ref[:, 7 * gw:9 * gw])
    pd = _dot(h, win_ref[:, 9 * gw:13 * gw])
    pa = _dot(h, win_ref[:, 0:4 * gw])

    qa = pa[:, 0:gw] * (HEAD_DIM ** -0.5 * LOG2E)
    ka = pa[:, gw:2 * gw]
    va = pa[:, 2 * gw:3 * gw]
    for s in range(N_SLABS):
        q_ref[0, s] = qa[:, s * LANE:(s + 1) * LANE]
        k_ref[0, s] = ka[:, s * LANE:(s + 1) * LANE]
        v_ref[0, s] = va[:, s * LANE:(s + 1) * LANE]
    ga_ref[0] = _silu(pa[:, 3 * gw:4 * gw]).astype(BF16)

    bu = pb[:, 0:gw]
    bv = pb[:, gw:2 * gw]
    bg = pb[:, 2 * gw:3 * gw]
    mu = jnp.mean(bv, axis=-1, keepdims=True)
    dv = bv - mu
    var = jnp.mean(dv * dv, axis=-1, keepdims=True)
    vn_s[...] = (dv * lax.rsqrt(var + NORM_EPS) * sgug_ref[...]).astype(BF16)
    causal = (lax.broadcasted_iota(jnp.int32, (CHUNK, CHUNK), 0)
              >= lax.broadcasted_iota(jnp.int32, (CHUNK, CHUNK), 1))
    w_cat = jnp.concatenate(
        [jnp.where(causal, sguw_ref[g], 0.0).astype(BF16) for g in range(N_HEADS)], axis=1)
    for c in range(n_chunks):
        vc = vn_s[c * CHUNK:(c + 1) * CHUNK, :]
        v_stack = jnp.concatenate(
            [jnp.where(lane_head == g, vc, jnp.zeros_like(vc)) for g in range(N_HEADS)], axis=0)
        mix_s[c * CHUNK:(c + 1) * CHUNK, :] = sgub_ref[...] + _dot(w_cat, v_stack)
    y_ref[0, :, 0:gw] = (bu * mix_s[...] * _silu(bg)).astype(BF16)

    cx = pc[:, 0:gw]
    cg = pc[:, gw:2 * gw]
    hist = POOL_HIST
    e0[hist:hist + tile, :] = cx
    s2[8:hist + tile, :] = e0[8:hist + tile, :] + e0[7:hist + tile - 1, :]
    s4[16:hist + tile, :] = s2[16:hist + tile, :] + s2[14:hist + tile - 2, :]
    s8[24:hist + tile, :] = s4[24:hist + tile, :] + s4[20:hist + tile - 4, :]
    w16 = s8[hist:hist + tile, :] + s8[hist - 8:hist + tile - 8, :]
    lane = lax.broadcasted_iota(jnp.int32, (1, gw), 1)
    win = jnp.where(lane < 64, s2[hist:hist + tile, :],
                    jnp.where(lane < 128, s4[hist:hist + tile, :],
                              jnp.where(lane < 192, s8[hist:hist + tile, :], w16)))
    p_lane = jnp.where(lane < 64, POOL_SIZES[0],
                       jnp.where(lane < 128, POOL_SIZES[1],
                                 jnp.where(lane < 192, POOL_SIZES[2], POOL_SIZES[3])))
    t_glob = n * tile + lax.broadcasted_iota(jnp.int32, (tile, 1), 0)
    cnt = jnp.minimum(t_glob + 1, p_lane).astype(F32)
    pooled = win / cnt - cx
    yc = _dot(pooled.astype(BF16), poolw_ref[...]) * pools_ref[...]
    y_ref[0, :, gw:2 * gw] = (yc * _silu(cg)).astype(BF16)
    e0[0:hist, :] = e0[tile:tile + hist, :]

    rq_s[...] = pd[:, 0:gw]
    rk_s[...] = pd[:, gw:2 * gw] * (HEAD_DIM ** -0.5)
    rv_s[...] = pd[:, 2 * gw:3 * gw]
    dg = pd[:, 3 * gw:4 * gw]
    row_head = lax.broadcasted_iota(jnp.int32, (gw, 1), 0) // HEAD_DIM
    same_head = row_head == lane_head
    seg = jnp.where(same_head, 1.0 / HEAD_DIM, 0.0).astype(BF16)

    st = state[...]
    for c in range(n_chunks):
        rs = slice(c * CHUNK, (c + 1) * CHUNK)
        qc = rq_s[rs, :]
        kc = rk_s[rs, :]
        vb = rv_s[rs, :].astype(BF16)
        kb = kc.astype(BF16)
        k_stack = jnp.concatenate(
            [jnp.where(lane_head == hh, kb, jnp.zeros_like(kb)) for hh in range(N_HEADS)], axis=0)
        v_stack = jnp.concatenate(
            [jnp.where(lane_head == hh, vb, jnp.zeros_like(vb)) for hh in range(N_HEADS)], axis=0)
        sc = _dot_nt(qc.astype(BF16), k_stack) * decay_ref[...]
        of = _dot(sc.astype(BF16), v_stack)
        of_s[rs, :] = of + _dot((qc * xi_ref[...]).astype(BF16), st.astype(BF16))
        kv = _dot_tn((kc * zeta_ref[...]).astype(BF16), vb)
        st = st * cdec_ref[...] + jnp.where(same_head, kv, 0.0)
    state[...] = st
    of = of_s[...]
    mu_h = _seg_mean(of, seg)
    dh = of - mu_h
    var_h = _dot((dh * dh).astype(BF16), seg)
    yd = dh * lax.rsqrt(var_h + NORM_EPS) * retg_ref[...]
    y_ref[0, :, 2 * gw:3 * gw] = (yd * _silu(dg)).astype(BF16)


def _inproj_mix(x, pre_g, w_in, sgu_g, sgu_w, sgu_bias, pool_w_bd, pool_scale, ret_g,
                decay, zeta, xi, cdec):
    b, s, d = x.shape
    tile = IN_TILE
    gw = GROUP_WIDTH
    const2 = lambda bi, ni: (0, 0)
    const3 = lambda bi, ni: (0, 0, 0)
    slab_spec = pl.BlockSpec((1, N_SLABS, tile, LANE), lambda bi, ni: (bi, 0, ni, 0))
    slab_shape = jax.ShapeDtypeStruct((b, N_SLABS, s, LANE), F32)
    return pl.pallas_call(
        _inproj_mix_kernel,
        grid=(b, s // tile),
        in_specs=[
            pl.BlockSpec((1, tile, d), lambda bi, ni: (bi, ni, 0)),
            pl.BlockSpec((1, d), const2),
            pl.BlockSpec(w_in.shape, const2),
            pl.BlockSpec((1, gw), const2),
            pl.BlockSpec(sgu_w.shape, const3),
            pl.BlockSpec((CHUNK, gw), const2),
            pl.BlockSpec((gw, gw), const2),
            pl.BlockSpec((1, gw), const2),
            pl.BlockSpec((1, gw), const2),
            pl.BlockSpec(decay.shape, const2),
            pl.BlockSpec((CHUNK, gw), const2),
            pl.BlockSpec((CHUNK, gw), const2),
            pl.BlockSpec((1, gw), const2),
        ],
        out_specs=[
            slab_spec, slab_spec, slab_spec,
            pl.BlockSpec((1, tile, gw), lambda bi, ni: (bi, ni, 0)),
            pl.BlockSpec((1, tile, 3 * gw), lambda bi, ni: (bi, ni, 0)),
        ],
        out_shape=[
            slab_shape, slab_shape, slab_shape,
            jax.ShapeDtypeStruct((b, s, gw), BF16),
            jax.ShapeDtypeStruct((b, s, 3 * gw), BF16),
        ],
        scratch_shapes=[
            pltpu.VMEM((tile + POOL_HIST, gw), F32),
            pltpu.VMEM((tile + POOL_HIST, gw), F32),
            pltpu.VMEM((tile + POOL_HIST, gw), F32),
            pltpu.VMEM((tile + POOL_HIST, gw), F32),
            pltpu.VMEM((gw, gw), F32),
            pltpu.VMEM((tile, gw), BF16),
            pltpu.VMEM((tile, gw), F32),
            pltpu.VMEM((tile, gw), F32),
            pltpu.VMEM((tile, gw), F32),
            pltpu.VMEM((tile, gw), F32),
            pltpu.VMEM((tile, gw), F32),
        ],
        compiler_params=pltpu.CompilerParams(
            dimension_semantics=("arbitrary", "arbitrary"), vmem_limit_bytes=VMEM_LIMIT),
        name="inproj_mix",
    )(x, pre_g, w_in, sgu_g, sgu_w, sgu_bias, pool_w_bd, pool_scale, ret_g, decay, zeta, xi, cdec)


def _dilated_attn_kernel(q_ref, k_ref, v_ref, ga_ref, o_ref, bias_s, ob_s, lb_s):
    bi = pl.program_id(0)
    qb = pl.program_id(1)
    base = qb * ATT_TILE
    lane_head = _lane_head()

    @pl.when((bi == 0) & (qb == 0))
    def _():
        ki = lax.broadcasted_iota(jnp.int32, (2 * CHUNK, CHUNK), 0)
        qi = lax.broadcasted_iota(jnp.int32, (2 * CHUNK, CHUNK), 1)
        dist = qi + CHUNK - ki
        band = (dist >= 0) & (dist <= CHUNK)
        distf = dist.astype(F32)
        for di, dil in enumerate(DILATIONS):
            for hh in range(N_HEADS):
                slope = 2.0 ** (-8.0 * (hh + 1) / N_HEADS)
                bias = -(slope * dil * LOG2E) * distf
                cs = pl.ds(hh * CHUNK, CHUNK)
                bias_s[0, di, :, cs] = jnp.where(band, bias, -jnp.inf)
                bias_s[1, di, :, cs] = jnp.where(band & (ki >= CHUNK), bias, -jnp.inf)

    def rows(ref, start, stride, dtype=BF16):
        idx = pl.ds(start, CHUNK) if stride == 1 else pl.ds(start, CHUNK, stride=stride)
        return jnp.concatenate([ref[0, s, idx, :] for s in range(N_SLABS)], axis=-1).astype(dtype)

    def attn_scores(di, q_start, q_local, first):
        dil = DILATIONS[di]
        span = dil * CHUNK
        cur = q_start
        prev = jnp.where(first, cur, cur - span)
        fi = first.astype(jnp.int32)
        qv = rows(q_ref, q_local, dil)
        kk = jnp.concatenate([rows(k_ref, prev, dil), rows(k_ref, cur, dil)], axis=0)
        vv = jnp.concatenate([rows(v_ref, prev, dil, F32), rows(v_ref, cur, dil, F32)], axis=0)
        v_t = vv.T.astype(BF16)
        qs = jnp.concatenate(
            [jnp.where(lane_head == hh, qv, jnp.zeros_like(qv)) for hh in range(N_HEADS)], axis=0)
        sc = _dot_nt(kk, qs) + bias_s[fi, di]
        return sc, v_t

    def attn_values(sc, v_t):
        m = jnp.max(sc, axis=0, keepdims=True)
        p = jnp.exp2(sc - m)
        l = jnp.sum(p, axis=0, keepdims=True)
        pb = p.astype(BF16)
        inv = 1.0 / l
        ls = m + jnp.log2(l)
        o_parts, lse_parts = [], []
        for hh in range(N_HEADS):
            cs = slice(hh * CHUNK, (hh + 1) * CHUNK)
            oh = _dot(v_t[hh * HEAD_DIM:(hh + 1) * HEAD_DIM, :], pb[:, cs])
            o_parts.append(oh * inv[:, cs])
            lse_parts.append(jnp.broadcast_to(ls[:, cs], (HEAD_DIM, CHUNK)))
        o_t = jnp.concatenate(o_parts, axis=0)
        lse_t = jnp.concatenate(lse_parts, axis=0)
        return o_t.T, lse_t.T

    def finish_strided(di, slot, q_local, staged):
        o, lse = attn_values(*staged)
        idx = pl.ds(q_local, CHUNK, stride=DILATIONS[di])
        for s in range(N_SLABS):
            ob_s[slot, s, idx, :] = o[:, s * LANE:(s + 1) * LANE]
            lb_s[slot, s, idx, :] = lse[:, s * LANE:(s + 1) * LANE]

    def finish_dense(di, slot, q_local, staged):
        o1, l1 = attn_values(*staged)
        idx = pl.ds(q_local, CHUNK)
        o4 = jnp.concatenate([ob_s[0, s, idx, :] for s in range(N_SLABS)], axis=-1)
        l4 = jnp.concatenate([lb_s[0, s, idx, :] for s in range(N_SLABS)], axis=-1)
        o16 = jnp.concatenate([ob_s[1, s, idx, :] for s in range(N_SLABS)], axis=-1)
        l16 = jnp.concatenate([lb_s[1, s, idx, :] for s in range(N_SLABS)], axis=-1)
        mx = jnp.maximum(jnp.maximum(l1, l4), l16)
        w1 = jnp.exp2(l1 - mx)
        w4 = jnp.exp2(l4 - mx)
        w16 = jnp.exp2(l16 - mx)
        o = (w1 * o1 + w4 * o4 + w16 * o16) / (w1 + w4 + w16)
        o_ref[0, idx, :] = (o * ga_ref[0, idx, :].astype(F32)).astype(BF16)

    def run_branch(di, slot, finish):
        span = DILATIONS[di] * CHUNK
        per_class = ATT_TILE // span

        def group(g, carry):
            in_flight = []
            for u in range(ATT_GROUP):
                i = g * ATT_GROUP + u
                nb = i % per_class
                q_local = i // per_class + nb * span
                if di == 0:
                    q_local = pl.multiple_of(q_local, CHUNK)
                first = (qb == 0) & (nb == 0)
                in_flight.append((q_local, attn_scores(di, base + q_local, q_local, first)))
                if len(in_flight) > ATT_LOOKAHEAD:
                    finish(di, slot, *in_flight.pop(0))
            for args in in_flight:
                finish(di, slot, *args)
            return carry

        lax.fori_loop(0, ATT_TILE // CHUNK // ATT_GROUP, group, 0)

    run_branch(1, 0, finish_strided)
    run_branch(2, 1, finish_strided)
    run_branch(0, None, finish_dense)


def _dilated_attn(q, k, v, ga):
    b, _, s, _ = q.shape
    gw = GROUP_WIDTH
    n_branch = len(DILATIONS)
    return pl.pallas_call(
        _dilated_attn_kernel,
        grid=(b, s // ATT_TILE),
        in_specs=[
            pl.BlockSpec((1, N_SLABS, ATT_TILE, LANE), lambda bi, qi: (bi, 0, qi, 0)),
            pl.BlockSpec((1, N_SLABS, s, LANE), lambda bi, qi: (bi, 0, 0, 0)),
            pl.BlockSpec((1, N_SLABS, s, LANE), lambda bi, qi: (bi, 0, 0, 0)),
            pl.BlockSpec((1, ATT_TILE, gw), lambda bi, qi: (bi, qi, 0)),
        ],
        out_specs=pl.BlockSpec((1, ATT_TILE, gw), lambda bi, qi: (bi, qi, 0)),
        out_shape=jax.ShapeDtypeStruct((b, s, gw), BF16),
        scratch_shapes=[
            pltpu.VMEM((2, n_branch, 2 * CHUNK, N_HEADS * CHUNK), F32),
            pltpu.VMEM((n_branch - 1, N_SLABS, ATT_TILE, LANE), F32),
            pltpu.VMEM((n_branch - 1, N_SLABS, ATT_TILE, LANE), F32),
        ],
        compiler_params=pltpu.CompilerParams(
            dimension_semantics=("arbitrary", "arbitrary"), vmem_limit_bytes=VMEM_LIMIT),
        name="dilated_attn",
    )(q, k, v, ga)


def _outproj_kernel(x_ref, ya_ref, y_ref, wout_ref, postg_ref, o_ref):
    gw = GROUP_WIDTH
    y = _dot(ya_ref[0], wout_ref[0:gw, :]) + _dot(y_ref[0], wout_ref[gw:, :])
    ms = jnp.mean(y * y, axis=-1, keepdims=True)
    o_ref[0] = x_ref[0] + y * lax.rsqrt(ms + NORM_EPS) * postg_ref[...]


def _outproj(x, ya, ybcd, w_out, post_g):
    b, s, d = x.shape
    tile = OUT_TILE
    gw = GROUP_WIDTH
    return pl.pallas_call(
        _outproj_kernel,
        grid=(b, s // tile),
        in_specs=[
            pl.BlockSpec((1, tile, d), lambda bi, ni: (bi, ni, 0)),
            pl.BlockSpec((1, tile, gw), lambda bi, ni: (bi, ni, 0)),
            pl.BlockSpec((1, tile, 3 * gw), lambda bi, ni: (bi, ni, 0)),
            pl.BlockSpec(w_out.shape, lambda bi, ni: (0, 0)),
            pl.BlockSpec((1, d), lambda bi, ni: (0, 0)),
        ],
        out_specs=pl.BlockSpec((1, tile, d), lambda bi, ni: (bi, ni, 0)),
        out_shape=jax.ShapeDtypeStruct(x.shape, x.dtype),
        compiler_params=pltpu.CompilerParams(
            dimension_semantics=("arbitrary", "arbitrary"), vmem_limit_bytes=VMEM_LIMIT),
        name="outproj",
    )(x, ya, ybcd, w_out, post_g)


def _retention_tables():
    log_g = jnp.log(1.0 - jnp.exp2(-5.0 - jnp.arange(N_HEADS, dtype=F32)))
    i = jnp.arange(CHUNK, dtype=F32)
    diff = i[:, None] - i[None, :]
    decay = jnp.where(diff >= 0, jnp.exp(log_g[:, None, None] * jnp.maximum(diff, 0.0)), 0.0)
    zeta = jnp.exp(log_g[:, None] * (CHUNK - 1 - i))
    xi = jnp.exp(log_g[:, None] * (i + 1))
    cdec = jnp.exp(log_g * CHUNK)
    per_lane = lambda t: jnp.repeat(t.T, HEAD_DIM, axis=1)
    decay = decay.transpose(1, 0, 2).reshape(CHUNK, N_HEADS * CHUNK)
    return decay, per_lane(zeta), per_lane(xi), jnp.repeat(cdec, HEAD_DIM)[None, :]


def _layer(x, pre_g, w_in, sgu_g, sgu_w, sgu_b, pool_w, pool_scale, ret_g, w_out, post_g, tables):
    decay, zeta, xi, cdec = tables
    sgu_bias = jnp.repeat(sgu_b.T, HEAD_DIM, axis=1)
    pool_w_bd = jax.scipy.linalg.block_diag(*[pool_w[g] for g in range(len(POOL_SIZES))]).astype(BF16)
    q, k, v, ga, ybcd = _inproj_mix(
        x, pre_g[None, :], w_in.astype(BF16), sgu_g[None, :], sgu_w, sgu_bias, pool_w_bd,
        pool_scale[None, :], ret_g[None, :], decay, zeta, xi, cdec)
    ya = _dilated_attn(q, k, v, ga)
    return _outproj(x, ya, ybcd, w_out.astype(BF16), post_g[None, :])


def kernel(x, pre_g, w_in, sgu_g, sgu_w, sgu_b, pool_w, pool_scale, ret_g, w_out, post_g):
    tables = _retention_tables()
    for l in range(pre_g.shape[0]):
        x = _layer(x, pre_g[l], w_in[l], sgu_g[l], sgu_w[l], sgu_b[l], pool_w[l], pool_scale[l],
                   ret_g[l], w_out[l], post_g[l], tables)
    return x
```

```python
import functools
import math

import jax
import jax.numpy as jnp
from jax import lax
from jax.experimental import pallas as pl
from jax.experimental.pallas import tpu as pltpu

F32 = jnp.float32
BF16 = jnp.bfloat16

HEAD_DIM = 64
N_HEADS = 4
GROUP_WIDTH = HEAD_DIM * N_HEADS
LANE = 128
N_SLABS = GROUP_WIDTH // LANE
CHUNK = 128
DILATIONS = (1, 4, 16)
POOL_SIZES = (2, 4, 8, 16)
POOL_HIST = 32
NORM_EPS = 1e-6
LOG2E = math.log2(math.e)
IN_TILE = 512
ATT_TILE = CHUNK * DILATIONS[-1]
OUT_SUB = 512
N_SUB = ATT_TILE // OUT_SUB
ATT_GROUP = 8
ATT_LOOKAHEAD = 1
VMEM_LIMIT = 56 * 1024 * 1024


def _silu(x):
    return x / (1.0 + jnp.exp(-x))


def _lane_head(width=GROUP_WIDTH):
    return lax.broadcasted_iota(jnp.int32, (1, width), 1) // HEAD_DIM


def _dot(a, b):
    return jnp.dot(a, b, preferred_element_type=F32)


def _dot_nt(a, b):
    return lax.dot_general(a, b, (((1,), (1,)), ((), ())), preferred_element_type=F32)


def _dot_tn(a, b):
    return lax.dot_general(a, b, (((0,), (0,)), ((), ())), preferred_element_type=F32)


def _seg_mean(a, seg):
    hi = a.astype(BF16)
    lo = (a - hi.astype(F32)).astype(BF16)
    return _dot(hi, seg) + _dot(lo, seg)


def _inproj_mix_kernel(x_ref, preg_ref, win_ref, sgug_ref, sguw_ref, sgub_ref, poolw_ref, pools_ref,
                       retg_ref, decay_ref, zeta_ref, xi_ref, cdec_ref,
                       q_ref, k_ref, v_ref, ga_ref, y_ref,
                       e0, s2, s4, s8, state, vn_s, mix_s, rq_s, rk_s, rv_s, of_s):
    n = pl.program_id(1)
    tile = x_ref.shape[1]
    n_chunks = tile // CHUNK
    gw = GROUP_WIDTH

    @pl.when(n == 0)
    def _():
        e0[0:POOL_HIST, :] = jnp.zeros((POOL_HIST, gw), F32)
        state[...] = jnp.zeros_like(state)

    x = x_ref[0]
    ms = jnp.mean(x * x, axis=-1, keepdims=True)
    h = (x * lax.rsqrt(ms + NORM_EPS) * preg_ref[...]).astype(BF16)

    lane_head = _lane_head()

    pb = _dot(h, win_ref[:, 4 * gw:7 * gw])
    pc = _dot(h, win_ref[:, 7 * gw:9 * gw])
    pd = _dot(h, win_ref[:, 9 * gw:13 * gw])
    pa = _dot(h, win_ref[:, 0:4 * gw])

    qa = pa[:, 0:gw] * (HEAD_DIM ** -0.5 * LOG2E)
    ka = pa[:, gw:2 * gw]
    va = pa[:, 2 * gw:3 * gw]
    for s in range(N_SLABS):
        q_ref[0, s] = qa[:, s * LANE:(s + 1) * LANE]
        k_ref[0, s] = ka[:, s * LANE:(s + 1) * LANE]
        v_ref[0, s] = va[:, s * LANE:(s + 1) * LANE]
    ga_ref[0] = _silu(pa[:, 3 * gw:4 * gw]).astype(BF16)

    bu = pb[:, 0:gw]
    bv = pb[:, gw:2 * gw]
    bg = pb[:, 2 * gw:3 * gw]
    mu = jnp.mean(bv, axis=-1, keepdims=True)
    dv = bv - mu
    var = jnp.mean(dv * dv, axis=-1, keepdims=True)
    vn_s[...] = (dv * lax.rsqrt(var + NORM_EPS) * sgug_ref[...]).astype(BF16)
    causal = (lax.broadcasted_iota(jnp.int32, (CHUNK, CHUNK), 0)
              >= lax.broadcasted_iota(jnp.int32, (CHUNK, CHUNK), 1))
    w_cat = jnp.concatenate(
        [jnp.where(causal, sguw_ref[g], 0.0).astype(BF16) for g in range(N_HEADS)], axis=1)
    for c in range(n_chunks):
        vc = vn_s[c * CHUNK:(c + 1) * CHUNK, :]
        v_stack = jnp.concatenate(
            [jnp.where(lane_head == g, vc, jnp.zeros_like(vc)) for g in range(N_HEADS)], axis=0)
        mix_s[c * CHUNK:(c + 1) * CHUNK, :] = sgub_ref[...] + _dot(w_cat, v_stack)
    y_ref[0, :, 0:gw] = (bu * mix_s[...] * _silu(bg)).astype(BF16)

    cx = pc[:, 0:gw]
    cg = pc[:, gw:2 * gw]
    hist = POOL_HIST
    e0[hist:hist + tile, :] = cx
    s2[8:hist + tile, :] = e0[8:hist + tile, :] + e0[7:hist + tile - 1, :]
    s4[16:hist + tile, :] = s2[16:hist + tile, :] + s2[14:hist + tile - 2, :]
    s8[24:hist + tile, :] = s4[24:hist + tile, :] + s4[20:hist + tile - 4, :]
    w16 = s8[hist:hist + tile, :] + s8[hist - 8:hist + tile - 8, :]
    lane = lax.broadcasted_iota(jnp.int32, (1, gw), 1)
    win = jnp.where(lane < 64, s2[hist:hist + tile, :],
                    jnp.where(lane < 128, s4[hist:hist + tile, :],
                              jnp.where(lane < 192, s8[hist:hist + tile, :], w16)))
    p_lane = jnp.where(lane < 64, POOL_SIZES[0],
                       jnp.where(lane < 128, POOL_SIZES[1],
                                 jnp.where(lane < 192, POOL_SIZES[2], POOL_SIZES[3])))
    t_glob = n * tile + lax.broadcasted_iota(jnp.int32, (tile, 1), 0)
    cnt = jnp.minimum(t_glob + 1, p_lane).astype(F32)
    pooled = win / cnt - cx
    yc = _dot(pooled.astype(BF16), poolw_ref[...]) * pools_ref[...]
    y_ref[0, :, gw:2 * gw] = (yc * _silu(cg)).astype(BF16)
    e0[0:hist, :] = e0[tile:tile + hist, :]

    rq_s[...] = pd[:, 0:gw]
    rk_s[...] = pd[:, gw:2 * gw] * (HEAD_DIM ** -0.5)
    rv_s[...] = pd[:, 2 * gw:3 * gw]
    dg = pd[:, 3 * gw:4 * gw]
    row_head = lax.broadcasted_iota(jnp.int32, (gw, 1), 0) // HEAD_DIM
    same_head = row_head == lane_head
    seg = jnp.where(same_head, 1.0 / HEAD_DIM, 0.0).astype(BF16)

    st = state[...]
    for c in range(n_chunks):
        rs = slice(c * CHUNK, (c + 1) * CHUNK)
        qc = rq_s[rs, :]
        kc = rk_s[rs, :]
        vb = rv_s[rs, :].astype(BF16)
        kb = kc.astype(BF16)
        k_stack = jnp.concatenate(
            [jnp.where(lane_head == hh, kb, jnp.zeros_like(kb)) for hh in range(N_HEADS)], axis=0)
        v_stack = jnp.concatenate(
            [jnp.where(lane_head == hh, vb, jnp.zeros_like(vb)) for hh in range(N_HEADS)], axis=0)
        sc = _dot_nt(qc.astype(BF16), k_stack) * decay_ref[...]
        of = _dot(sc.astype(BF16), v_stack)
        of_s[rs, :] = of + _dot((qc * xi_ref[...]).astype(BF16), st.astype(BF16))
        kv = _dot_tn((kc * zeta_ref[...]).astype(BF16), vb)
        st = st * cdec_ref[...] + jnp.where(same_head, kv, 0.0)
    state[...] = st
    of = of_s[...]
    mu_h = _seg_mean(of, seg)
    dh = of - mu_h
    var_h = _dot((dh * dh).astype(BF16), seg)
    yd = dh * lax.rsqrt(var_h + NORM_EPS) * retg_ref[...]
    y_ref[0, :, 2 * gw:3 * gw] = (yd * _silu(dg)).astype(BF16)


def _inproj_mix(x, pre_g, w_in, sgu_g, sgu_w, sgu_bias, pool_w_bd, pool_scale, ret_g,
                decay, zeta, xi, cdec):
    b, s, d = x.shape
    tile = IN_TILE
    gw = GROUP_WIDTH
    const2 = lambda bi, ni: (0, 0)
    const3 = lambda bi, ni: (0, 0, 0)
    slab_spec = pl.BlockSpec((1, N_SLABS, tile, LANE), lambda bi, ni: (bi, 0, ni, 0))
    slab_shape = jax.ShapeDtypeStruct((b, N_SLABS, s, LANE), F32)
    return pl.pallas_call(
        _inproj_mix_kernel,
        grid=(b, s // tile),
        in_specs=[
            pl.BlockSpec((1, tile, d), lambda bi, ni: (bi, ni, 0)),
            pl.BlockSpec((1, d), const2),
            pl.BlockSpec(w_in.shape, const2),
            pl.BlockSpec((1, gw), const2),
            pl.BlockSpec(sgu_w.shape, const3),
            pl.BlockSpec((CHUNK, gw), const2),
            pl.BlockSpec((gw, gw), const2),
            pl.BlockSpec((1, gw), const2),
            pl.BlockSpec((1, gw), const2),
            pl.BlockSpec(decay.shape, const2),
            pl.BlockSpec((CHUNK, gw), const2),
            pl.BlockSpec((CHUNK, gw), const2),
            pl.BlockSpec((1, gw), const2),
        ],
        out_specs=[
            slab_spec, slab_spec, slab_spec,
            pl.BlockSpec((1, tile, gw), lambda bi, ni: (bi, ni, 0)),
            pl.BlockSpec((1, tile, 3 * gw), lambda bi, ni: (bi, ni, 0)),
        ],
        out_shape=[
            slab_shape, slab_shape, slab_shape,
            jax.ShapeDtypeStruct((b, s, gw), BF16),
            jax.ShapeDtypeStruct((b, s, 3 * gw), BF16),
        ],
        scratch_shapes=[
            pltpu.VMEM((tile + POOL_HIST, gw), F32),
            pltpu.VMEM((tile + POOL_HIST, gw), F32),
            pltpu.VMEM((tile + POOL_HIST, gw), F32),
            pltpu.VMEM((tile + POOL_HIST, gw), F32),
            pltpu.VMEM((gw, gw), F32),
            pltpu.VMEM((tile, gw), BF16),
            pltpu.VMEM((tile, gw), F32),
            pltpu.VMEM((tile, gw), F32),
            pltpu.VMEM((tile, gw), F32),
            pltpu.VMEM((tile, gw), F32),
            pltpu.VMEM((tile, gw), F32),
        ],
        compiler_params=pltpu.CompilerParams(
            dimension_semantics=("arbitrary", "arbitrary"), vmem_limit_bytes=VMEM_LIMIT),
        name="inproj_mix",
    )(x, pre_g, w_in, sgu_g, sgu_w, sgu_bias, pool_w_bd, pool_scale, ret_g, decay, zeta, xi, cdec)


def _attn_outproj_kernel(q_ref, k_ref, v_ref, ga_ref, wout_ref, postg_ref, x_hbm, y_hbm, out_hbm,
                         bias_s, ob_s, lb_s, ya_s, xbuf, ybuf, obuf, yacc, in_sem, out_sem):
    t = pl.program_id(0)
    n_tiles = pl.num_programs(0) - 1
    tiles_per_seq = k_ref.shape[2] // ATT_TILE
    qb = jnp.minimum(t, n_tiles - 1) % tiles_per_seq
    base = qb * ATT_TILE
    lane_head = _lane_head()
    gw = GROUP_WIDTH

    has_prev = t >= 1
    prev_tile = jnp.maximum(t - 1, 0)
    prev_b = prev_tile // tiles_per_seq
    prev_row0 = (prev_tile % tiles_per_seq) * ATT_TILE

    def in_copies(b_idx, row, slot):
        rows = pl.ds(row, OUT_SUB)
        return (pltpu.make_async_copy(x_hbm.at[b_idx, rows, :], xbuf.at[slot], in_sem.at[0, slot]),
                pltpu.make_async_copy(y_hbm.at[b_idx, rows, :], ybuf.at[slot], in_sem.at[1, slot]))

    def out_copy(j, slot):
        rows = pl.ds(prev_row0 + j * OUT_SUB, OUT_SUB)
        return pltpu.make_async_copy(obuf.at[slot], out_hbm.at[prev_b, rows, :], out_sem.at[slot])

    def sub_begin(j):
        slot = j % 2

        @pl.when(has_prev)
        def _():
            for c in in_copies(prev_b, prev_row0 + j * OUT_SUB, slot):
                c.wait()

            @pl.when(j + 1 < N_SUB)
            def _():
                for c in in_copies(prev_b, prev_row0 + (j + 1) * OUT_SUB, 1 - slot):
                    c.start()

            @pl.when(j >= 2)
            def _():
                out_copy(j - 2, slot).wait()

    def sub_end(j):
        @pl.when(has_prev)
        def _():
            out_copy(j, j % 2).start()

    n_col = wout_ref.shape[1] // gw
    n_row = OUT_SUB // CHUNK
    pieces_per_block = n_row * n_col // ATT_GROUP

    def outproj_piece(j, p):
        rb, c = divmod(p, n_col)
        rows = slice(rb * CHUNK, (rb + 1) * CHUNK)
        cols = slice(c * gw, (c + 1) * gw)
        ya_rows = pl.ds(pl.multiple_of(j * OUT_SUB + rb * CHUNK, CHUNK), CHUNK)
        yacc[rows, cols] = (_dot(ya_s[ya_rows, :], wout_ref[0:gw, cols])
                            + _dot(ybuf[j % 2, rows, :], wout_ref[gw:, cols]))
        if c == n_col - 1:
            y = yacc[rows, :]
            ms = jnp.mean(y * y, axis=-1, keepdims=True)
            obuf[j % 2, rows, :] = xbuf[j % 2, rows, :] + y * lax.rsqrt(ms + NORM_EPS) * postg_ref[...]

    @pl.when(t == 0)
    def _():
        ya_s[...] = jnp.zeros_like(ya_s)
        xbuf[...] = jnp.zeros_like(xbuf)
        ybuf[...] = jnp.zeros_like(ybuf)
        ki = lax.broadcasted_iota(jnp.int32, (2 * CHUNK, CHUNK), 0)
        qi = lax.broadcasted_iota(jnp.int32, (2 * CHUNK, CHUNK), 1)
        dist = qi + CHUNK - ki
        band = (dist >= 0) & (dist <= CHUNK)
        distf = dist.astype(F32)
        for di, dil in enumerate(DILATIONS):
            for hh in range(N_HEADS):
                slope = 2.0 ** (-8.0 * (hh + 1) / N_HEADS)
                bias = -(slope * dil * LOG2E) * distf
                cs = pl.ds(hh * CHUNK, CHUNK)
                bias_s[0, di, :, cs] = jnp.where(band, bias, -jnp.inf)
                bias_s[1, di, :, cs] = jnp.where(band & (ki >= CHUNK), bias, -jnp.inf)

    def rows(ref, start, stride, dtype=BF16):
        idx = pl.ds(start, CHUNK) if stride == 1 else pl.ds(start, CHUNK, stride=stride)
        return jnp.concatenate([ref[0, s, idx, :] for s in range(N_SLABS)], axis=-1).astype(dtype)

    def attn_scores(di, q_start, q_local, first):
        dil = DILATIONS[di]
        span = dil * CHUNK
        cur = q_start
        prev = jnp.where(first, cur, cur - span)
        fi = first.astype(jnp.int32)
        qv = rows(q_ref, q_local, dil)
        kk = jnp.concatenate([rows(k_ref, prev, dil), rows(k_ref, cur, dil)], axis=0)
        vv = jnp.concatenate([rows(v_ref, prev, dil, F32), rows(v_ref, cur, dil, F32)], axis=0)
        v_t = vv.T.astype(BF16)
        qs = jnp.concatenate(
            [jnp.where(lane_head == hh, qv, jnp.zeros_like(qv)) for hh in range(N_HEADS)], axis=0)
        sc = _dot_nt(kk, qs) + bias_s[fi, di]
        return sc, v_t

    def attn_values(sc, v_t):
        o_parts, lse_parts = [], []
        for hh in range(N_HEADS):
            s_h = sc[:, hh * CHUNK:(hh + 1) * CHUNK]
            m = jnp.max(s_h, axis=0, keepdims=True)
            p = jnp.exp2(s_h - m)
            l = jnp.sum(p, axis=0, keepdims=True)
            oh = _dot(v_t[hh * HEAD_DIM:(hh + 1) * HEAD_DIM, :], p.astype(BF16))
            o_parts.append(oh * (1.0 / l))
            lse_parts.append(jnp.broadcast_to(m + jnp.log2(l), (HEAD_DIM, CHUNK)))
        o_t = jnp.concatenate(o_parts, axis=0)
        lse_t = jnp.concatenate(lse_parts, axis=0)
        return o_t.T, lse_t.T

    def finish_strided(di, slot, q_local, staged):
        o, lse = attn_values(*staged)
        idx = pl.ds(q_local, CHUNK, stride=DILATIONS[di])
        for s in range(N_SLABS):
            ob_s[slot, s, idx, :] = o[:, s * LANE:(s + 1) * LANE]
            lb_s[slot, s, idx, :] = lse[:, s * LANE:(s + 1) * LANE]

    def finish_dense(di, slot, q_local, staged):
        o1, l1 = attn_values(*staged)
        idx = pl.ds(q_local, CHUNK)
        o4 = jnp.concatenate([ob_s[0, s, idx, :] for s in range(N_SLABS)], axis=-1)
        l4 = jnp.concatenate([lb_s[0, s, idx, :] for s in range(N_SLABS)], axis=-1)
        o16 = jnp.concatenate([ob_s[1, s, idx, :] for s in range(N_SLABS)], axis=-1)
        l16 = jnp.concatenate([lb_s[1, s, idx, :] for s in range(N_SLABS)], axis=-1)
        mx = jnp.maximum(jnp.maximum(l1, l4), l16)
        w1 = jnp.exp2(l1 - mx)
        w4 = jnp.exp2(l4 - mx)
        w16 = jnp.exp2(l16 - mx)
        o = (w1 * o1 + w4 * o4 + w16 * o16) / (w1 + w4 + w16)
        ya_s[idx, :] = (o * ga_ref[0, idx, :].astype(F32)).astype(BF16)

    def run_branch(di, slot, finish, sub0=None):
        span = DILATIONS[di] * CHUNK
        per_class = ATT_TILE // span

        def group(g, carry):
            if sub0 is not None:
                sub_begin(sub0 + g)
            in_flight = []
            for u in range(ATT_GROUP):
                i = g * ATT_GROUP + u
                nb = i % per_class
                q_local = i // per_class + nb * span
                if di == 0:
                    q_local = pl.multiple_of(q_local, CHUNK)
                first = (qb == 0) & (nb == 0)
                in_flight.append((q_local, attn_scores(di, base + q_local, q_local, first)))
                if sub0 is not None:
                    for p in range(u * pieces_per_block, (u + 1) * pieces_per_block):
                        outproj_piece(sub0 + g, p)
                if len(in_flight) > ATT_LOOKAHEAD:
                    finish(di, slot, *in_flight.pop(0))
            for args in in_flight:
                finish(di, slot, *args)
            if sub0 is not None:
                sub_end(sub0 + g)
            return carry

        lax.fori_loop(0, ATT_TILE // CHUNK // ATT_GROUP, group, 0)

    groups_per_branch = ATT_TILE // CHUNK // ATT_GROUP

    @pl.when(t < n_tiles)
    def _():
        run_branch(1, 0, finish_strided, sub0=0)
        run_branch(2, 1, finish_strided, sub0=groups_per_branch)
        run_branch(0, None, finish_dense)

    @pl.when(t == n_tiles)
    def _():
        for j in range(N_SUB):
            sub_begin(j)
            for p in range(n_row * n_col):
                outproj_piece(j, p)
            sub_end(j)

    @pl.when(has_prev)
    def _():
        for j in range(N_SUB - 2, N_SUB):
            out_copy(j, j % 2).wait()

    @pl.when(t < n_tiles)
    def _():
        for c in in_copies(t // tiles_per_seq, (t % tiles_per_seq) * ATT_TILE, 0):
            c.start()


def _attn_outproj(q, k, v, ga, w_out, post_g, x, ybcd):
    b, _, s, _ = q.shape
    d = x.shape[-1]
    gw = GROUP_WIDTH
    n_branch = len(DILATIONS)
    tiles_per_seq = s // ATT_TILE
    n_tiles = b * tiles_per_seq
    assert 2 * (ATT_TILE // CHUNK // ATT_GROUP) == N_SUB

    def tile_of(t):
        ta = jnp.minimum(t, n_tiles - 1)
        return ta // tiles_per_seq, ta % tiles_per_seq

    return pl.pallas_call(
        _attn_outproj_kernel,
        grid=(n_tiles + 1,),
        in_specs=[
            pl.BlockSpec((1, N_SLABS, ATT_TILE, LANE), lambda t: (tile_of(t)[0], 0, tile_of(t)[1], 0)),
            pl.BlockSpec((1, N_SLABS, s, LANE), lambda t: (tile_of(t)[0], 0, 0, 0)),
            pl.BlockSpec((1, N_SLABS, s, LANE), lambda t: (tile_of(t)[0], 0, 0, 0)),
            pl.BlockSpec((1, ATT_TILE, gw), lambda t: (tile_of(t)[0], tile_of(t)[1], 0)),
            pl.BlockSpec(w_out.shape, lambda t: (0, 0)),
            pl.BlockSpec((1, d), lambda t: (0, 0)),
            pl.BlockSpec(memory_space=pl.ANY),
            pl.BlockSpec(memory_space=pl.ANY),
        ],
        out_specs=pl.BlockSpec(memory_space=pl.ANY),
        out_shape=jax.ShapeDtypeStruct(x.shape, x.dtype),
        scratch_shapes=[
            pltpu.VMEM((2, n_branch, 2 * CHUNK, N_HEADS * CHUNK), F32),
            pltpu.VMEM((n_branch - 1, N_SLABS, ATT_TILE, LANE), F32),
            pltpu.VMEM((n_branch - 1, N_SLABS, ATT_TILE, LANE), F32),
            pltpu.VMEM((ATT_TILE, gw), BF16),
            pltpu.VMEM((2, OUT_SUB, d), F32),
            pltpu.VMEM((2, OUT_SUB, 3 * gw), BF16),
            pltpu.VMEM((2, OUT_SUB, d), F32),
            pltpu.VMEM((OUT_SUB, d), F32),
            pltpu.SemaphoreType.DMA((2, 2)),
            pltpu.SemaphoreType.DMA((2,)),
        ],
        compiler_params=pltpu.CompilerParams(
            dimension_semantics=("arbitrary",), vmem_limit_bytes=VMEM_LIMIT),
        name="attn_outproj",
    )(q, k, v, ga, w_out, post_g, x, ybcd)


def _retention_tables():
    log_g = jnp.log(1.0 - jnp.exp2(-5.0 - jnp.arange(N_HEADS, dtype=F32)))
    i = jnp.arange(CHUNK, dtype=F32)
    diff = i[:, None] - i[None, :]
    decay = jnp.where(diff >= 0, jnp.exp(log_g[:, None, None] * jnp.maximum(diff, 0.0)), 0.0)
    zeta = jnp.exp(log_g[:, None] * (CHUNK - 1 - i))
    xi = jnp.exp(log_g[:, None] * (i + 1))
    cdec = jnp.exp(log_g * CHUNK)
    per_lane = lambda t: jnp.repeat(t.T, HEAD_DIM, axis=1)
    decay = decay.transpose(1, 0, 2).reshape(CHUNK, N_HEADS * CHUNK)
    return decay, per_lane(zeta), per_lane(xi), jnp.repeat(cdec, HEAD_DIM)[None, :]


def _layer(x, pre_g, w_in, sgu_g, sgu_w, sgu_b, pool_w, pool_scale, ret_g, w_out, post_g, tables):
    decay, zeta, xi, cdec = tables
    sgu_bias = jnp.repeat(sgu_b.T, HEAD_DIM, axis=1)
    pool_w_bd = jax.scipy.linalg.block_diag(*[pool_w[g] for g in range(len(POOL_SIZES))]).astype(BF16)
    q, k, v, ga, ybcd = _inproj_mix(
        x, pre_g[None, :], w_in.astype(BF16), sgu_g[None, :], sgu_w, sgu_bias, pool_w_bd,
        pool_scale[None, :], ret_g[None, :], decay, zeta, xi, cdec)
    return _attn_outproj(q, k, v, ga, w_out.astype(BF16), post_g[None, :], x, ybcd)


def kernel(x, pre_g, w_in, sgu_g, sgu_w, sgu_b, pool_w, pool_scale, ret_g, w_out, post_g):
    tables = _retention_tables()
    for l in range(pre_g.shape[0]):
        x = _layer(x, pre_g[l], w_in[l], sgu_g[l], sgu_w[l], sgu_b[l], pool_w[l], pool_scale[l],
                   ret_g[l], w_out[l], post_g[l], tables)
    return x
```

```python
import functools
import math

import jax
import jax.numpy as jnp
from jax import lax
from jax.experimental import pallas as pl
from jax.experimental.pallas import tpu as pltpu

F32 = jnp.float32
BF16 = jnp.bfloat16

HEAD_DIM = 64
N_HEADS = 4
GROUP_WIDTH = HEAD_DIM * N_HEADS
LANE = 128
N_SLABS = GROUP_WIDTH // LANE
CHUNK = 128
DILATIONS = (1, 4, 16)
POOL_SIZES = (2, 4, 8, 16)
POOL_HIST = 32
NORM_EPS = 1e-6
LOG2E = math.log2(math.e)
IN_TILE = 512
OUT_TILE = 1024
ATT_TILE = CHUNK * DILATIONS[-1]
ATT_GROUP = 8
ATT_LOOKAHEAD = 1
VMEM_LIMIT = 56 * 1024 * 1024


def _silu(x):
    return x / (1.0 + jnp.exp(-x))


def _lane_head(width=GROUP_WIDTH):
    return lax.broadcasted_iota(jnp.int32, (1, width), 1) // HEAD_DIM


def _dot(a, b):
    return jnp.dot(a, b, preferred_element_type=F32)


def _dot_nt(a, b):
    return lax.dot_general(a, b, (((1,), (1,)), ((), ())), preferred_element_type=F32)


def _dot_tn(a, b):
    return lax.dot_general(a, b, (((0,), (0,)), ((), ())), preferred_element_type=F32)


def _seg_mean(a, seg):
    hi = a.astype(BF16)
    lo = (a - hi.astype(F32)).astype(BF16)
    return _dot(hi, seg) + _dot(lo, seg)


def _inproj_mix_kernel(x_ref, preg_ref, win_ref, sgug_ref, sguw_ref, sgub_ref, poolw_ref, pools_ref,
                       retg_ref, decay_ref, zeta_ref, xi_ref, cdec_ref,
                       q_ref, k_ref, v_ref, ga_ref, y_ref,
                       e0, s2, s4, s8, state, vn_s, mix_s, rq_s, rk_s, rv_s, of_s):
    n = pl.program_id(1)
    tile = x_ref.shape[1]
    n_chunks = tile // CHUNK
    gw = GROUP_WIDTH

    @pl.when(n == 0)
    def _():
        e0[0:POOL_HIST, :] = jnp.zeros((POOL_HIST, gw), F32)
        state[...] = jnp.zeros_like(state)

    x = x_ref[0]
    ms = jnp.mean(x * x, axis=-1, keepdims=True)
    h = (x * lax.rsqrt(ms + NORM_EPS) * preg_ref[...]).astype(BF16)

    lane_head = _lane_head()

    pb = _dot(h, win_ref[:, 4 * gw:7 * gw])
    pc = _dot(h, win_ref[:, 7 * gw:9 * gw])
    pd = _dot(h, win_ref[:, 9 * gw:13 * gw])
    pa = _dot(h, win_ref[:, 0:4 * gw])

    qa = pa[:, 0:gw] * (HEAD_DIM ** -0.5 * LOG2E)
    ka = pa[:, gw:2 * gw]
    va = pa[:, 2 * gw:3 * gw]
    for s in range(N_SLABS):
        q_ref[0, s] = qa[:, s * LANE:(s + 1) * LANE]
        k_ref[0, s] = ka[:, s * LANE:(s + 1) * LANE]
        v_ref[0, s] = va[:, s * LANE:(s + 1) * LANE]
    ga_ref[0] = _silu(pa[:, 3 * gw:4 * gw]).astype(BF16)

    bu = pb[:, 0:gw]
    bv = pb[:, gw:2 * gw]
    bg = pb[:, 2 * gw:3 * gw]
    mu = jnp.mean(bv, axis=-1, keepdims=True)
    dv = bv - mu
    var = jnp.mean(dv * dv, axis=-1, keepdims=True)
    vn_s[...] = (dv * lax.rsqrt(var + NORM_EPS) * sgug_ref[...]).astype(BF16)
    causal = (lax.broadcasted_iota(jnp.int32, (CHUNK, CHUNK), 0)
              >= lax.broadcasted_iota(jnp.int32, (CHUNK, CHUNK), 1))
    w_cat = jnp.concatenate(
        [jnp.where(causal, sguw_ref[g], 0.0).astype(BF16) for g in range(N_HEADS)], axis=1)
    for c in range(n_chunks):
        vc = vn_s[c * CHUNK:(c + 1) * CHUNK, :]
        v_stack = jnp.concatenate(
            [jnp.where(lane_head == g, vc, jnp.zeros_like(vc)) for g in range(N_HEADS)], axis=0)
        mix_s[c * CHUNK:(c + 1) * CHUNK, :] = sgub_ref[...] + _dot(w_cat, v_stack)
    y_ref[0, :, 0:gw] = (bu * mix_s[...] * _silu(bg)).astype(BF16)

    cx = pc[:, 0:gw]
    cg = pc[:, gw:2 * gw]
    hist = POOL_HIST
    e0[hist:hist + tile, :] = cx
    s2[8:hist + tile, :] = e0[8:hist + tile, :] + e0[7:hist + tile - 1, :]
    s4[16:hist + tile, :] = s2[16:hist + tile, :] + s2[14:hist + tile - 2, :]
    s8[24:hist + tile, :] = s4[24:hist + tile, :] + s4[20:hist + tile - 4, :]
    w16 = s8[hist:hist + tile, :] + s8[hist - 8:hist + tile - 8, :]
    lane = lax.broadcasted_iota(jnp.int32, (1, gw), 1)
    win = jnp.where(lane < 64, s2[hist:hist + tile, :],
                    jnp.where(lane < 128, s4[hist:hist + tile, :],
                              jnp.where(lane < 192, s8[hist:hist + tile, :], w16)))
    p_lane = jnp.where(lane < 64, POOL_SIZES[0],
                       jnp.where(lane < 128, POOL_SIZES[1],
                                 jnp.where(lane < 192, POOL_SIZES[2], POOL_SIZES[3])))
    t_glob = n * tile + lax.broadcasted_iota(jnp.int32, (tile, 1), 0)
    cnt = jnp.minimum(t_glob + 1, p_lane).astype(F32)
    pooled = win / cnt - cx
    yc = _dot(pooled.astype(BF16), poolw_ref[...]) * pools_ref[...]
    y_ref[0, :, gw:2 * gw] = (yc * _silu(cg)).astype(BF16)
    e0[0:hist, :] = e0[tile:tile + hist, :]

    rq_s[...] = pd[:, 0:gw]
    rk_s[...] = pd[:, gw:2 * gw] * (HEAD_DIM ** -0.5)
    rv_s[...] = pd[:, 2 * gw:3 * gw]
    dg = pd[:, 3 * gw:4 * gw]
    row_head = lax.broadcasted_iota(jnp.int32, (gw, 1), 0) // HEAD_DIM
    same_head = row_head == lane_head
    seg = jnp.where(same_head, 1.0 / HEAD_DIM, 0.0).astype(BF16)

    st = state[...]
    for c in range(n_chunks):
        rs = slice(c * CHUNK, (c + 1) * CHUNK)
        qc = rq_s[rs, :]
        kc = rk_s[rs, :]
        vb = rv_s[rs, :].astype(BF16)
        kb = kc.astype(BF16)
        k_stack = jnp.concatenate(
            [jnp.where(lane_head == hh, kb, jnp.zeros_like(kb)) for hh in range(N_HEADS)], axis=0)
        v_stack = jnp.concatenate(
            [jnp.where(lane_head == hh, vb, jnp.zeros_like(vb)) for hh in range(N_HEADS)], axis=0)
        sc = _dot_nt(qc.astype(BF16), k_stack) * decay_ref[...]
        of = _dot(sc.astype(BF16), v_stack)
        of_s[rs, :] = of + _dot((qc * xi_ref[...]).astype(BF16), st.astype(BF16))
        kv = _dot_tn((kc * zeta_ref[...]).astype(BF16), vb)
        st = st * cdec_ref[...] + jnp.where(same_head, kv, 0.0)
    state[...] = st
    of = of_s[...]
    mu_h = _seg_mean(of, seg)
    dh = of - mu_h
    var_h = _dot((dh * dh).astype(BF16), seg)
    yd = dh * lax.rsqrt(var_h + NORM_EPS) * retg_ref[...]
    y_ref[0, :, 2 * gw:3 * gw] = (yd * _silu(dg)).astype(BF16)


def _inproj_mix(x, pre_g, w_in, sgu_g, sgu_w, sgu_bias, pool_w_bd, pool_scale, ret_g,
                decay, zeta, xi, cdec):
    b, s, d = x.shape
    tile = IN_TILE
    gw = GROUP_WIDTH
    const2 = lambda bi, ni: (0, 0)
    const3 = lambda bi, ni: (0, 0, 0)
    slab_spec = pl.BlockSpec((1, N_SLABS, tile, LANE), lambda bi, ni: (bi, 0, ni, 0))
    slab_shape = jax.ShapeDtypeStruct((b, N_SLABS, s, LANE), F32)
    return pl.pallas_call(
        _inproj_mix_kernel,
        grid=(b, s // tile),
        in_specs=[
            pl.BlockSpec((1, tile, d), lambda bi, ni: (bi, ni, 0)),
            pl.BlockSpec((1, d), const2),
            pl.BlockSpec(w_in.shape, const2),
            pl.BlockSpec((1, gw), const2),
            pl.BlockSpec(sgu_w.shape, const3),
            pl.BlockSpec((CHUNK, gw), const2),
            pl.BlockSpec((gw, gw), const2),
            pl.BlockSpec((1, gw), const2),
            pl.BlockSpec((1, gw), const2),
            pl.BlockSpec(decay.shape, const2),
            pl.BlockSpec((CHUNK, gw), const2),
            pl.BlockSpec((CHUNK, gw), const2),
            pl.BlockSpec((1, gw), const2),
        ],
        out_specs=[
            slab_spec, slab_spec, slab_spec,
            pl.BlockSpec((1, tile, gw), lambda bi, ni: (bi, ni, 0)),
            pl.BlockSpec((1, tile, 3 * gw), lambda bi, ni: (bi, ni, 0)),
        ],
        out_shape=[
            slab_shape, slab_shape, slab_shape,
            jax.ShapeDtypeStruct((b, s, gw), BF16),
            jax.ShapeDtypeStruct((b, s, 3 * gw), BF16),
        ],
        scratch_shapes=[
            pltpu.VMEM((tile + POOL_HIST, gw), F32),
            pltpu.VMEM((tile + POOL_HIST, gw), F32),
            pltpu.VMEM((tile + POOL_HIST, gw), F32),
            pltpu.VMEM((tile + POOL_HIST, gw), F32),
            pltpu.VMEM((gw, gw), F32),
            pltpu.VMEM((tile, gw), BF16),
            pltpu.VMEM((tile, gw), F32),
            pltpu.VMEM((tile, gw), F32),
            pltpu.VMEM((tile, gw), F32),
            pltpu.VMEM((tile, gw), F32),
            pltpu.VMEM((tile, gw), F32),
        ],
        compiler_params=pltpu.CompilerParams(
            dimension_semantics=("arbitrary", "arbitrary"), vmem_limit_bytes=VMEM_LIMIT),
        name="inproj_mix",
    )(x, pre_g, w_in, sgu_g, sgu_w, sgu_bias, pool_w_bd, pool_scale, ret_g, decay, zeta, xi, cdec)


def _dilated_attn_kernel(q_ref, k_ref, v_ref, ga_ref, o_ref, bias_s, ob_s, lb_s, kc_s, vc_s):
    bi = pl.program_id(0)
    qb = pl.program_id(1)
    base = qb * ATT_TILE
    lane_head = _lane_head()

    @pl.when((bi == 0) & (qb == 0))
    def _():
        kc_s[...] = jnp.zeros_like(kc_s)
        vc_s[...] = jnp.zeros_like(vc_s)
        ki = lax.broadcasted_iota(jnp.int32, (2 * CHUNK, CHUNK), 0)
        qi = lax.broadcasted_iota(jnp.int32, (2 * CHUNK, CHUNK), 1)
        dist = qi + CHUNK - ki
        band = (dist >= 0) & (dist <= CHUNK)
        distf = dist.astype(F32)
        for di, dil in enumerate(DILATIONS):
            for hh in range(N_HEADS):
                slope = 2.0 ** (-8.0 * (hh + 1) / N_HEADS)
                bias = -(slope * dil * LOG2E) * distf
                cs = pl.ds(hh * CHUNK, CHUNK)
                bias_s[0, di, :, cs] = jnp.where(band, bias, -jnp.inf)
                bias_s[1, di, :, cs] = jnp.where(band & (ki >= CHUNK), bias, -jnp.inf)

    def rows(ref, start, stride, dtype=BF16):
        idx = pl.ds(start, CHUNK) if stride == 1 else pl.ds(start, CHUNK, stride=stride)
        return jnp.concatenate([ref[0, s, idx, :] for s in range(N_SLABS)], axis=-1).astype(dtype)

    def key_block(di, start):
        dil = DILATIONS[di]
        return rows(k_ref, start, dil), rows(v_ref, start, dil, F32).T.astype(BF16)

    def attn_scores(di, q_start, q_local, first, prev_keys):
        dil = DILATIONS[di]
        span = dil * CHUNK
        cur = q_start
        fi = first.astype(jnp.int32)
        if prev_keys is None:
            prev_keys = key_block(di, jnp.where(first, cur, cur - span))
        cur_keys = key_block(di, cur)
        qv = rows(q_ref, q_local, dil)
        kk = jnp.concatenate([prev_keys[0], cur_keys[0]], axis=0)
        v_t = jnp.concatenate([prev_keys[1], cur_keys[1]], axis=1)
        qs = jnp.concatenate(
            [jnp.where(lane_head == hh, qv, jnp.zeros_like(qv)) for hh in range(N_HEADS)], axis=0)
        sc = _dot_nt(kk, qs) + bias_s[fi, di]
        return (sc, v_t), cur_keys

    def attn_values(sc, v_t):
        m = jnp.max(sc, axis=0, keepdims=True)
        p = jnp.exp2(sc - m)
        l = jnp.sum(p, axis=0, keepdims=True)
        pb = p.astype(BF16)
        inv = 1.0 / l
        ls = m + jnp.log2(l)
        o_parts, lse_parts = [], []
        for hh in range(N_HEADS):
            cs = slice(hh * CHUNK, (hh + 1) * CHUNK)
            oh = _dot(v_t[hh * HEAD_DIM:(hh + 1) * HEAD_DIM, :], pb[:, cs])
            o_parts.append(oh * inv[:, cs])
            lse_parts.append(jnp.broadcast_to(ls[:, cs], (HEAD_DIM, CHUNK)))
        o_t = jnp.concatenate(o_parts, axis=0)
        lse_t = jnp.concatenate(lse_parts, axis=0)
        return o_t.T, lse_t.T

    def finish_strided(di, slot, q_local, staged):
        o, lse = attn_values(*staged)
        idx = pl.ds(q_local, CHUNK, stride=DILATIONS[di])
        for s in range(N_SLABS):
            ob_s[slot, s, idx, :] = o[:, s * LANE:(s + 1) * LANE]
            lb_s[slot, s, idx, :] = lse[:, s * LANE:(s + 1) * LANE]

    def finish_dense(di, slot, q_local, staged):
        o1, l1 = attn_values(*staged)
        idx = pl.ds(q_local, CHUNK)
        o4 = jnp.concatenate([ob_s[0, s, idx, :] for s in range(N_SLABS)], axis=-1)
        l4 = jnp.concatenate([lb_s[0, s, idx, :] for s in range(N_SLABS)], axis=-1)
        o16 = jnp.concatenate([ob_s[1, s, idx, :] for s in range(N_SLABS)], axis=-1)
        l16 = jnp.concatenate([lb_s[1, s, idx, :] for s in range(N_SLABS)], axis=-1)
        mx = jnp.maximum(jnp.maximum(l1, l4), l16)
        w1 = jnp.exp2(l1 - mx)
        w4 = jnp.exp2(l4 - mx)
        w16 = jnp.exp2(l16 - mx)
        o = (w1 * o1 + w4 * o4 + w16 * o16) / (w1 + w4 + w16)
        o_ref[0, idx, :] = (o * ga_ref[0, idx, :].astype(F32)).astype(BF16)

    def run_branch(di, slot, finish):
        span = DILATIONS[di] * CHUNK
        per_class = ATT_TILE // span

        def group(g, carry):
            in_flight = []
            keys = None
            for u in range(ATT_GROUP):
                i = g * ATT_GROUP + u
                nb = i % per_class
                q_local = i // per_class + nb * span
                if di == 0:
                    q_local = pl.multiple_of(q_local, CHUNK)
                first = (qb == 0) & (nb == 0)
                if per_class == 1:
                    keys = (kc_s[i], vc_s[i])
                elif u % min(per_class, ATT_GROUP) == 0:
                    keys = None
                staged, keys = attn_scores(di, base + q_local, q_local, first, keys)
                if per_class == 1:
                    kc_s[i], vc_s[i] = keys
                in_flight.append((q_local, staged))
                if len(in_flight) > ATT_LOOKAHEAD:
                    finish(di, slot, *in_flight.pop(0))
            for args in in_flight:
                finish(di, slot, *args)
            return carry

        lax.fori_loop(0, ATT_TILE // CHUNK // ATT_GROUP, group, 0)

    run_branch(1, 0, finish_strided)
    run_branch(2, 1, finish_strided)
    run_branch(0, None, finish_dense)


def _dilated_attn(q, k, v, ga):
    b, _, s, _ = q.shape
    gw = GROUP_WIDTH
    n_branch = len(DILATIONS)
    return pl.pallas_call(
        _dilated_attn_kernel,
        grid=(b, s // ATT_TILE),
        in_specs=[
            pl.BlockSpec((1, N_SLABS, ATT_TILE, LANE), lambda bi, qi: (bi, 0, qi, 0)),
            pl.BlockSpec((1, N_SLABS, s, LANE), lambda bi, qi: (bi, 0, 0, 0)),
            pl.BlockSpec((1, N_SLABS, s, LANE), lambda bi, qi: (bi, 0, 0, 0)),
            pl.BlockSpec((1, ATT_TILE, gw), lambda bi, qi: (bi, qi, 0)),
        ],
        out_specs=pl.BlockSpec((1, ATT_TILE, gw), lambda bi, qi: (bi, qi, 0)),
        out_shape=jax.ShapeDtypeStruct((b, s, gw), BF16),
        scratch_shapes=[
            pltpu.VMEM((2, n_branch, 2 * CHUNK, N_HEADS * CHUNK), F32),
            pltpu.VMEM((n_branch - 1, N_SLABS, ATT_TILE, LANE), F32),
            pltpu.VMEM((n_branch - 1, N_SLABS, ATT_TILE, LANE), F32),
            pltpu.VMEM((DILATIONS[-1], CHUNK, gw), BF16),
            pltpu.VMEM((DILATIONS[-1], gw, CHUNK), BF16),
        ],
        compiler_params=pltpu.CompilerParams(
            dimension_semantics=("arbitrary", "arbitrary"), vmem_limit_bytes=VMEM_LIMIT),
        name="dilated_attn",
    )(q, k, v, ga)


def _outproj_kernel(x_ref, ya_ref, y_ref, wout_ref, postg_ref, o_ref):
    gw = GROUP_WIDTH
    y = _dot(ya_ref[0], wout_ref[0:gw, :]) + _dot(y_ref[0], wout_ref[gw:, :])
    ms = jnp.mean(y * y, axis=-1, keepdims=True)
    o_ref[0] = x_ref[0] + y * lax.rsqrt(ms + NORM_EPS) * postg_ref[...]


def _outproj(x, ya, ybcd, w_out, post_g):
    b, s, d = x.shape
    tile = OUT_TILE
    gw = GROUP_WIDTH
    return pl.pallas_call(
        _outproj_kernel,
        grid=(b, s // tile),
        in_specs=[
            pl.BlockSpec((1, tile, d), lambda bi, ni: (bi, ni, 0)),
            pl.BlockSpec((1, tile, gw), lambda bi, ni: (bi, ni, 0)),
            pl.BlockSpec((1, tile, 3 * gw), lambda bi, ni: (bi, ni, 0)),
            pl.BlockSpec(w_out.shape, lambda bi, ni: (0, 0)),
            pl.BlockSpec((1, d), lambda bi, ni: (0, 0)),
        ],
        out_specs=pl.BlockSpec((1, tile, d), lambda bi, ni: (bi, ni, 0)),
        out_shape=jax.ShapeDtypeStruct(x.shape, x.dtype),
        compiler_params=pltpu.CompilerParams(
            dimension_semantics=("arbitrary", "arbitrary"), vmem_limit_bytes=VMEM_LIMIT),
        name="outproj",
    )(x, ya, ybcd, w_out, post_g)


def _retention_tables():
    log_g = jnp.log(1.0 - jnp.exp2(-5.0 - jnp.arange(N_HEADS, dtype=F32)))
    i = jnp.arange(CHUNK, dtype=F32)
    diff = i[:, None] - i[None, :]
    decay = jnp.where(diff >= 0, jnp.exp(log_g[:, None, None] * jnp.maximum(diff, 0.0)), 0.0)
    zeta = jnp.exp(log_g[:, None] * (CHUNK - 1 - i))
    xi = jnp.exp(log_g[:, None] * (i + 1))
    cdec = jnp.exp(log_g * CHUNK)
    per_lane = lambda t: jnp.repeat(t.T, HEAD_DIM, axis=1)
    decay = decay.transpose(1, 0, 2).reshape(CHUNK, N_HEADS * CHUNK)
    return decay, per_lane(zeta), per_lane(xi), jnp.repeat(cdec, HEAD_DIM)[None, :]


def _layer(x, pre_g, w_in, sgu_g, sgu_w, sgu_b, pool_w, pool_scale, ret_g, w_out, post_g, tables):
    decay, zeta, xi, cdec = tables
    sgu_bias = jnp.repeat(sgu_b.T, HEAD_DIM, axis=1)
    pool_w_bd = jax.scipy.linalg.block_diag(*[pool_w[g] for g in range(len(POOL_SIZES))]).astype(BF16)
    q, k, v, ga, ybcd = _inproj_mix(
        x, pre_g[None, :], w_in.astype(BF16), sgu_g[None, :], sgu_w, sgu_bias, pool_w_bd,
        pool_scale[None, :], ret_g[None, :], decay, zeta, xi, cdec)
    ya = _dilated_attn(q, k, v, ga)
    return _outproj(x, ya, ybcd, w_out.astype(BF16), post_g[None, :])


def kernel(x, pre_g, w_in, sgu_g, sgu_w, sgu_b, pool_w, pool_scale, ret_g, w_out, post_g):
    tables = _retention_tables()
    for l in range(pre_g.shape[0]):
        x = _layer(x, pre_g[l], w_in[l], sgu_g[l], sgu_w[l], sgu_b[l], pool_w[l], pool_scale[l],
                   ret_g[l], w_out[l], post_g[l], tables)
    return x
```

```python
import math

import jax
import numpy as np
import jax.numpy as jnp
from jax import lax
from jax.experimental import pallas as pl
from jax.experimental.pallas import tpu as pltpu

F32 = jnp.float32
BF16 = jnp.bfloat16

HEAD_DIM = 64
N_HEADS = 4
GROUP_WIDTH = HEAD_DIM * N_HEADS
LANE = 128
N_SLABS = GROUP_WIDTH // LANE
CHUNK = 128
DILATIONS = (1, 4, 16)
POOL_SIZES = (2, 4, 8, 16)
POOL_HIST = 32
NORM_EPS = 1e-6
LOG2E = math.log2(math.e)
IN_TILE = 512
OUT_TILE = 1024
ATT_TILE = CHUNK * DILATIONS[-1]
ATT_GROUP = 8
ATT_LOOKAHEAD = 1
VMEM_LIMIT = 56 * 1024 * 1024


def _silu(x):
    return x / (1.0 + jnp.exp(-x))


def _lane_head(width=GROUP_WIDTH):
    return lax.broadcasted_iota(jnp.int32, (1, width), 1) // HEAD_DIM


def _dot(a, b):
    return jnp.dot(a, b, preferred_element_type=F32)


def _dot_nt(a, b):
    return lax.dot_general(a, b, (((1,), (1,)), ((), ())), preferred_element_type=F32)


def _dot_tn(a, b):
    return lax.dot_general(a, b, (((0,), (0,)), ((), ())), preferred_element_type=F32)


def _seg_mean(a, seg):
    hi = a.astype(BF16)
    lo = (a - hi.astype(F32)).astype(BF16)
    return _dot(hi, seg) + _dot(lo, seg)


def _inproj_mix_kernel(x_ref, preg_ref, win_ref, sgug_ref, sguw_ref, sgub_ref, poolw_ref, pools_ref,
                       retg_ref, decay_ref, zeta_ref, xi_ref, cdec_ref,
                       q_ref, k_ref, v_ref, ga_ref, y_ref,
                       e0, s2, s4, s8, state, vn_s, mix_s, rq_s, rk_s, rv_s, of_s):
    n = pl.program_id(1)
    tile = x_ref.shape[1]
    n_chunks = tile // CHUNK
    gw = GROUP_WIDTH

    @pl.when(n == 0)
    def _():
        e0[0:POOL_HIST, :] = jnp.zeros((POOL_HIST, gw), F32)
        state[...] = jnp.zeros_like(state)

    x = x_ref[0]
    ms = jnp.mean(x * x, axis=-1, keepdims=True)
    h = (x * lax.rsqrt(ms + NORM_EPS) * preg_ref[...]).astype(BF16)

    lane_head = _lane_head()

    pb = _dot(h, win_ref[:, 4 * gw:7 * gw])
    pc = _dot(h, win_ref[:, 7 * gw:9 * gw])
    pd = _dot(h, win_ref[:, 9 * gw:13 * gw])

    bu = pb[:, 0:gw]
    bv = pb[:, gw:2 * gw]
    bg = pb[:, 2 * gw:3 * gw]
    mu = jnp.mean(bv, axis=-1, keepdims=True)
    dv = bv - mu
    var = jnp.mean(dv * dv, axis=-1, keepdims=True)
    vn_s[...] = (dv * lax.rsqrt(var + NORM_EPS) * sgug_ref[...]).astype(BF16)
    causal = (lax.broadcasted_iota(jnp.int32, (CHUNK, CHUNK), 0)
              >= lax.broadcasted_iota(jnp.int32, (CHUNK, CHUNK), 1))
    w_cat = jnp.concatenate(
        [jnp.where(causal, sguw_ref[g], 0.0).astype(BF16) for g in range(N_HEADS)], axis=1)
    for c in range(n_chunks):
        vc = vn_s[c * CHUNK:(c + 1) * CHUNK, :]
        v_stack = jnp.concatenate(
            [jnp.where(lane_head == g, vc, jnp.zeros_like(vc)) for g in range(N_HEADS)], axis=0)
        mix_s[c * CHUNK:(c + 1) * CHUNK, :] = sgub_ref[...] + _dot(w_cat, v_stack)
    y_ref[0, :, 0:gw] = (bu * mix_s[...] * _silu(bg)).astype(BF16)

    cx = pc[:, 0:gw]
    cg = pc[:, gw:2 * gw]
    hist = POOL_HIST
    e0[hist:hist + tile, :] = cx
    s2[8:hist + tile, :] = e0[8:hist + tile, :] + e0[7:hist + tile - 1, :]
    s4[16:hist + tile, :] = s2[16:hist + tile, :] + s2[14:hist + tile - 2, :]
    s8[24:hist + tile, :] = s4[24:hist + tile, :] + s4[20:hist + tile - 4, :]
    w16 = s8[hist:hist + tile, :] + s8[hist - 8:hist + tile - 8, :]
    lane = lax.broadcasted_iota(jnp.int32, (1, gw), 1)
    win = jnp.where(lane < 64, s2[hist:hist + tile, :],
                    jnp.where(lane < 128, s4[hist:hist + tile, :],
                              jnp.where(lane < 192, s8[hist:hist + tile, :], w16)))
    p_lane = jnp.where(lane < 64, POOL_SIZES[0],
                       jnp.where(lane < 128, POOL_SIZES[1],
                                 jnp.where(lane < 192, POOL_SIZES[2], POOL_SIZES[3])))
    t_glob = n * tile + lax.broadcasted_iota(jnp.int32, (tile, 1), 0)
    cnt = jnp.minimum(t_glob + 1, p_lane).astype(F32)
    pooled = win / cnt - cx
    yc = _dot(pooled.astype(BF16), poolw_ref[...]) * pools_ref[...]
    y_ref[0, :, gw:2 * gw] = (yc * _silu(cg)).astype(BF16)
    e0[0:hist, :] = e0[tile:tile + hist, :]

    rq_s[...] = pd[:, 0:gw]
    rk_s[...] = pd[:, gw:2 * gw] * (HEAD_DIM ** -0.5)
    rv_s[...] = pd[:, 2 * gw:3 * gw]
    dg = pd[:, 3 * gw:4 * gw]
    row_head = lax.broadcasted_iota(jnp.int32, (gw, 1), 0) // HEAD_DIM
    same_head = row_head == lane_head
    seg = jnp.where(same_head, 1.0 / HEAD_DIM, 0.0).astype(BF16)

    st = state[...]
    for c in range(n_chunks):
        rs = slice(c * CHUNK, (c + 1) * CHUNK)
        qc = rq_s[rs, :]
        kc = rk_s[rs, :]
        vb = rv_s[rs, :].astype(BF16)
        kb = kc.astype(BF16)
        k_stack = jnp.concatenate(
            [jnp.where(lane_head == hh, kb, jnp.zeros_like(kb)) for hh in range(N_HEADS)], axis=0)
        v_stack = jnp.concatenate(
            [jnp.where(lane_head == hh, vb, jnp.zeros_like(vb)) for hh in range(N_HEADS)], axis=0)
        sc = _dot_nt(qc.astype(BF16), k_stack) * decay_ref[...]
        of = _dot(sc.astype(BF16), v_stack)
        of_s[rs, :] = of + _dot((qc * xi_ref[...]).astype(BF16), st.astype(BF16))
        kv = _dot_tn((kc * zeta_ref[...]).astype(BF16), vb)
        st = st * cdec_ref[...] + jnp.where(same_head, kv, 0.0)
    state[...] = st
    of = of_s[...]
    mu_h = _seg_mean(of, seg)
    dh = of - mu_h
    var_h = _dot((dh * dh).astype(BF16), seg)
    yd = dh * lax.rsqrt(var_h + NORM_EPS) * retg_ref[...]
    y_ref[0, :, 2 * gw:3 * gw] = (yd * _silu(dg)).astype(BF16)

    pa = _dot(h, win_ref[:, 0:4 * gw])
    qa = pa[:, 0:gw] * (HEAD_DIM ** -0.5 * LOG2E)
    ka = pa[:, gw:2 * gw]
    va = pa[:, 2 * gw:3 * gw]
    for s in range(N_SLABS):
        q_ref[0, s] = qa[:, s * LANE:(s + 1) * LANE]
        k_ref[0, s] = ka[:, s * LANE:(s + 1) * LANE]
        v_ref[0, s] = va[:, s * LANE:(s + 1) * LANE]
    ga_ref[0] = _silu(pa[:, 3 * gw:4 * gw]).astype(BF16)


def _layer_spec(layer, shape):
    return pl.BlockSpec((None,) + tuple(shape), lambda bi, ni: (layer,) + (0,) * len(shape))


def _inproj_mix(layer, x, pre_g, w_in, sgu_g, sgu_w, sgu_bias, pool_w_bd, pool_scale, ret_g,
                decay, zeta, xi, cdec):
    b, s, d = x.shape
    tile = IN_TILE
    gw = GROUP_WIDTH
    const2 = lambda bi, ni: (0, 0)
    slab_spec = pl.BlockSpec((1, N_SLABS, tile, LANE), lambda bi, ni: (bi, 0, ni, 0))
    slab_shape = jax.ShapeDtypeStruct((b, N_SLABS, s, LANE), F32)
    return pl.pallas_call(
        _inproj_mix_kernel,
        grid=(b, s // tile),
        in_specs=[
            pl.BlockSpec((1, tile, d), lambda bi, ni: (bi, ni, 0)),
            _layer_spec(layer, (1, d)),
            _layer_spec(layer, w_in.shape[1:]),
            _layer_spec(layer, (1, gw)),
            _layer_spec(layer, sgu_w.shape[1:]),
            _layer_spec(layer, (CHUNK, gw)),
            _layer_spec(layer, (gw, gw)),
            _layer_spec(layer, (1, gw)),
            _layer_spec(layer, (1, gw)),
            pl.BlockSpec(decay.shape, const2),
            pl.BlockSpec((CHUNK, gw), const2),
            pl.BlockSpec((CHUNK, gw), const2),
            pl.BlockSpec((1, gw), const2),
        ],
        out_specs=[
            slab_spec, slab_spec, slab_spec,
            pl.BlockSpec((1, tile, gw), lambda bi, ni: (bi, ni, 0)),
            pl.BlockSpec((1, tile, 3 * gw), lambda bi, ni: (bi, ni, 0)),
        ],
        out_shape=[
            slab_shape, slab_shape, slab_shape,
            jax.ShapeDtypeStruct((b, s, gw), BF16),
            jax.ShapeDtypeStruct((b, s, 3 * gw), BF16),
        ],
        scratch_shapes=[
            pltpu.VMEM((tile + POOL_HIST, gw), F32),
            pltpu.VMEM((tile + POOL_HIST, gw), F32),
            pltpu.VMEM((tile + POOL_HIST, gw), F32),
            pltpu.VMEM((tile + POOL_HIST, gw), F32),
            pltpu.VMEM((gw, gw), F32),
            pltpu.VMEM((tile, gw), BF16),
            pltpu.VMEM((tile, gw), F32),
            pltpu.VMEM((tile, gw), F32),
            pltpu.VMEM((tile, gw), F32),
            pltpu.VMEM((tile, gw), F32),
            pltpu.VMEM((tile, gw), F32),
        ],
        compiler_params=pltpu.CompilerParams(
            dimension_semantics=("arbitrary", "arbitrary"), vmem_limit_bytes=VMEM_LIMIT),
        name="inproj_mix",
    )(x, pre_g, w_in, sgu_g, sgu_w, sgu_bias, pool_w_bd, pool_scale, ret_g, decay, zeta, xi, cdec)


def _dilated_attn_kernel(q_ref, k_ref, v_ref, ga_ref, o_ref, bias_s, ob_s, lb_s, kc_s, vc_s):
    bi = pl.program_id(0)
    qb = pl.program_id(1)
    base = qb * ATT_TILE
    lane_head = _lane_head()

    @pl.when((bi == 0) & (qb == 0))
    def _():
        kc_s[...] = jnp.zeros_like(kc_s)
        vc_s[...] = jnp.zeros_like(vc_s)
        ki = lax.broadcasted_iota(jnp.int32, (2 * CHUNK, CHUNK), 0)
        qi = lax.broadcasted_iota(jnp.int32, (2 * CHUNK, CHUNK), 1)
        dist = qi + CHUNK - ki
        band = (dist >= 0) & (dist <= CHUNK)
        distf = dist.astype(F32)
        for di, dil in enumerate(DILATIONS):
            for hh in range(N_HEADS):
                slope = 2.0 ** (-8.0 * (hh + 1) / N_HEADS)
                bias = -(slope * dil * LOG2E) * distf
                cs = pl.ds(hh * CHUNK, CHUNK)
                bias_s[0, di, :, cs] = jnp.where(band, bias, -jnp.inf)
                bias_s[1, di, :, cs] = jnp.where(band & (ki >= CHUNK), bias, -jnp.inf)

    def rows(ref, start, stride, dtype=BF16):
        idx = pl.ds(start, CHUNK) if stride == 1 else pl.ds(start, CHUNK, stride=stride)
        return jnp.concatenate([ref[0, s, idx, :] for s in range(N_SLABS)], axis=-1).astype(dtype)

    def key_block(di, start):
        dil = DILATIONS[di]
        return rows(k_ref, start, dil), rows(v_ref, start, dil, F32).T.astype(BF16)

    def attn_scores(di, q_start, q_local, first, prev_keys):
        dil = DILATIONS[di]
        span = dil * CHUNK
        cur = q_start
        fi = first.astype(jnp.int32)
        if prev_keys is None:
            prev_keys = key_block(di, jnp.where(first, cur, cur - span))
        cur_keys = key_block(di, cur)
        qv = rows(q_ref, q_local, dil)
        kk = jnp.concatenate([prev_keys[0], cur_keys[0]], axis=0)
        v_t = jnp.concatenate([prev_keys[1], cur_keys[1]], axis=1)
        qs = jnp.concatenate(
            [jnp.where(lane_head == hh, qv, jnp.zeros_like(qv)) for hh in range(N_HEADS)], axis=0)
        sc = _dot_nt(kk, qs) + bias_s[fi, di]
        return (sc, v_t), cur_keys

    def attn_values(sc, v_t):
        m = jnp.max(sc, axis=0, keepdims=True)
        p = jnp.exp2(sc - m)
        l = jnp.sum(p, axis=0, keepdims=True)
        pb = p.astype(BF16)
        inv = 1.0 / l
        ls = m + jnp.log2(l)
        o_parts, lse_parts = [], []
        for hh in range(N_HEADS):
            cs = slice(hh * CHUNK, (hh + 1) * CHUNK)
            oh = _dot(v_t[hh * HEAD_DIM:(hh + 1) * HEAD_DIM, :], pb[:, cs])
            o_parts.append(oh * inv[:, cs])
            lse_parts.append(jnp.broadcast_to(ls[:, cs], (HEAD_DIM, CHUNK)))
        o_t = jnp.concatenate(o_parts, axis=0)
        lse_t = jnp.concatenate(lse_parts, axis=0)
        return o_t.T, lse_t.T

    def finish_strided(di, slot, q_local, staged):
        o, lse = attn_values(*staged)
        idx = pl.ds(q_local, CHUNK, stride=DILATIONS[di])
        for s in range(N_SLABS):
            ob_s[slot, s, idx, :] = o[:, s * LANE:(s + 1) * LANE]
            lb_s[slot, s, idx, :] = lse[:, s * LANE:(s + 1) * LANE]

    def finish_dense(di, slot, q_local, staged):
        o1, l1 = attn_values(*staged)
        idx = pl.ds(q_local, CHUNK)
        o4 = jnp.concatenate([ob_s[0, s, idx, :] for s in range(N_SLABS)], axis=-1)
        l4 = jnp.concatenate([lb_s[0, s, idx, :] for s in range(N_SLABS)], axis=-1)
        o16 = jnp.concatenate([ob_s[1, s, idx, :] for s in range(N_SLABS)], axis=-1)
        l16 = jnp.concatenate([lb_s[1, s, idx, :] for s in range(N_SLABS)], axis=-1)
        mx = jnp.maximum(jnp.maximum(l1, l4), l16)
        w1 = jnp.exp2(l1 - mx)
        w4 = jnp.exp2(l4 - mx)
        w16 = jnp.exp2(l16 - mx)
        o = (w1 * o1 + w4 * o4 + w16 * o16) / (w1 + w4 + w16)
        o_ref[0, idx, :] = (o * ga_ref[0, idx, :].astype(F32)).astype(BF16)

    def run_branch(di, slot, finish):
        span = DILATIONS[di] * CHUNK
        per_class = ATT_TILE // span

        def group(g, carry):
            in_flight = []
            keys = None
            for u in range(ATT_GROUP):
                i = g * ATT_GROUP + u
                nb = i % per_class
                q_local = i // per_class + nb * span
                if di == 0:
                    q_local = pl.multiple_of(q_local, CHUNK)
                first = (qb == 0) & (nb == 0)
                if per_class == 1:
                    keys = (kc_s[i], vc_s[i])
                elif u % min(per_class, ATT_GROUP) == 0:
                    keys = None
                staged, keys = attn_scores(di, base + q_local, q_local, first, keys)
                if per_class == 1:
                    kc_s[i], vc_s[i] = keys
                in_flight.append((q_local, staged))
                if len(in_flight) > ATT_LOOKAHEAD:
                    finish(di, slot, *in_flight.pop(0))
            for args in in_flight:
                finish(di, slot, *args)
            return carry

        lax.fori_loop(0, ATT_TILE // CHUNK // ATT_GROUP, group, 0)

    run_branch(1, 0, finish_strided)
    run_branch(2, 1, finish_strided)
    run_branch(0, None, finish_dense)


def _dilated_attn(q, k, v, ga):
    b, _, s, _ = q.shape
    gw = GROUP_WIDTH
    n_branch = len(DILATIONS)
    return pl.pallas_call(
        _dilated_attn_kernel,
        grid=(b, s // ATT_TILE),
        in_specs=[
            pl.BlockSpec((1, N_SLABS, ATT_TILE, LANE), lambda bi, qi: (bi, 0, qi, 0)),
            pl.BlockSpec((1, N_SLABS, s, LANE), lambda bi, qi: (bi, 0, 0, 0)),
            pl.BlockSpec((1, N_SLABS, s, LANE), lambda bi, qi: (bi, 0, 0, 0)),
            pl.BlockSpec((1, ATT_TILE, gw), lambda bi, qi: (bi, qi, 0)),
        ],
        out_specs=pl.BlockSpec((1, ATT_TILE, gw), lambda bi, qi: (bi, qi, 0)),
        out_shape=jax.ShapeDtypeStruct((b, s, gw), BF16),
        scratch_shapes=[
            pltpu.VMEM((2, n_branch, 2 * CHUNK, N_HEADS * CHUNK), F32),
            pltpu.VMEM((n_branch - 1, N_SLABS, ATT_TILE, LANE), F32),
            pltpu.VMEM((n_branch - 1, N_SLABS, ATT_TILE, LANE), F32),
            pltpu.VMEM((DILATIONS[-1], CHUNK, gw), BF16),
            pltpu.VMEM((DILATIONS[-1], gw, CHUNK), BF16),
        ],
        compiler_params=pltpu.CompilerParams(
            dimension_semantics=("arbitrary", "arbitrary"), vmem_limit_bytes=VMEM_LIMIT),
        name="dilated_attn",
    )(q, k, v, ga)


def _outproj_kernel(x_ref, ya_ref, y_ref, wout_ref, postg_ref, o_ref):
    gw = GROUP_WIDTH
    y = _dot(ya_ref[0], wout_ref[0:gw, :]) + _dot(y_ref[0], wout_ref[gw:, :])
    ms = jnp.mean(y * y, axis=-1, keepdims=True)
    o_ref[0] = x_ref[0] + y * lax.rsqrt(ms + NORM_EPS) * postg_ref[...]


def _outproj(layer, x, ya, ybcd, w_out, post_g):
    b, s, d = x.shape
    tile = OUT_TILE
    gw = GROUP_WIDTH
    return pl.pallas_call(
        _outproj_kernel,
        grid=(b, s // tile),
        in_specs=[
            pl.BlockSpec((1, tile, d), lambda bi, ni: (bi, ni, 0)),
            pl.BlockSpec((1, tile, gw), lambda bi, ni: (bi, ni, 0)),
            pl.BlockSpec((1, tile, 3 * gw), lambda bi, ni: (bi, ni, 0)),
            _layer_spec(layer, w_out.shape[1:]),
            _layer_spec(layer, (1, d)),
        ],
        out_specs=pl.BlockSpec((1, tile, d), lambda bi, ni: (bi, ni, 0)),
        out_shape=jax.ShapeDtypeStruct(x.shape, x.dtype),
        compiler_params=pltpu.CompilerParams(
            dimension_semantics=("arbitrary", "arbitrary"), vmem_limit_bytes=VMEM_LIMIT),
        name="outproj",
    )(x, ya, ybcd, w_out, post_g)


def _retention_tables():
    f32 = np.float32
    log_g = np.log(f32(1.0) - np.exp2(f32(-5.0) - np.arange(N_HEADS, dtype=f32)))
    i = np.arange(CHUNK, dtype=f32)
    diff = i[:, None] - i[None, :]
    decay = np.where(diff >= 0, np.exp(log_g[:, None, None] * np.maximum(diff, f32(0.0))), f32(0.0))
    zeta = np.exp(log_g[:, None] * (f32(CHUNK - 1) - i))
    xi = np.exp(log_g[:, None] * (i + f32(1.0)))
    cdec = np.exp(log_g * f32(CHUNK))
    per_lane = lambda t: np.repeat(t.T, HEAD_DIM, axis=1)
    decay = decay.transpose(1, 0, 2).reshape(CHUNK, N_HEADS * CHUNK)
    tables = decay, per_lane(zeta), per_lane(xi), np.repeat(cdec, HEAD_DIM)[None, :]
    return tuple(jnp.asarray(t, F32) for t in tables)


def kernel(x, pre_g, w_in, sgu_g, sgu_w, sgu_b, pool_w, pool_scale, ret_g, w_out, post_g):
    row = lambda p: p[:, None, :]
    sgu_bias = jnp.repeat(jnp.swapaxes(sgu_b, 1, 2), HEAD_DIM, axis=2)
    eye = jnp.eye(len(POOL_SIZES), dtype=pool_w.dtype)
    pool_w_bd = jnp.einsum('lgcd,gh->lgchd', pool_w, eye).reshape(
        pool_w.shape[0], GROUP_WIDTH, GROUP_WIDTH).astype(BF16)
    w_in_b, w_out_b = w_in.astype(BF16), w_out.astype(BF16)
    tables = _retention_tables()
    for layer in range(pre_g.shape[0]):
        q, k, v, ga, ybcd = _inproj_mix(
            layer, x, row(pre_g), w_in_b, row(sgu_g), sgu_w, sgu_bias, pool_w_bd, row(pool_scale),
            row(ret_g), *tables)
        ya = _dilated_attn(q, k, v, ga)
        x = _outproj(layer, x, ya, ybcd, w_out_b, row(post_g))
    return x
```

```python
import math

import jax
import numpy as np
import jax.numpy as jnp
from jax import lax
from jax.experimental import pallas as pl
from jax.experimental.pallas import tpu as pltpu

F32 = jnp.float32
BF16 = jnp.bfloat16

HEAD_DIM = 64
N_HEADS = 4
GROUP_WIDTH = HEAD_DIM * N_HEADS
LANE = 128
N_SLABS = GROUP_WIDTH // LANE
CHUNK = 128
DILATIONS = (1, 4, 16)
POOL_SIZES = (2, 4, 8, 16)
POOL_HIST = 32
NORM_EPS = 1e-6
LOG2E = math.log2(math.e)
IN_TILE = 512
OUT_TILE = 2048
ATT_TILE = CHUNK * DILATIONS[-1]
ONES_ROWS = 16
ATT_GROUP = 16
ATT_LOOKAHEAD = 2
VMEM_LIMIT = 56 * 1024 * 1024


def _silu(x):
    return x / (1.0 + jnp.exp(-x))


def _lane_head(width=GROUP_WIDTH):
    return lax.broadcasted_iota(jnp.int32, (1, width), 1) // HEAD_DIM


def _dot(a, b):
    return jnp.dot(a, b, preferred_element_type=F32)


def _dot_nt(a, b):
    return lax.dot_general(a, b, (((1,), (1,)), ((), ())), preferred_element_type=F32)


def _dot_tn(a, b):
    return lax.dot_general(a, b, (((0,), (0,)), ((), ())), preferred_element_type=F32)


def _seg_mean(a, seg):
    hi = a.astype(BF16)
    lo = (a - hi.astype(F32)).astype(BF16)
    return _dot(hi, seg) + _dot(lo, seg)


def _inproj_mix_kernel(x_ref, preg_ref, win_ref, sgug_ref, sguw_ref, sgub_ref, poolw_ref, pools_ref,
                       retg_ref, decay_ref, zeta_ref, xi_ref, cdec_ref,
                       q_ref, k_ref, v_ref, ga_ref, y_ref,
                       e0, s2, s4, s8, state, vn_s, mix_s, rq_s, rk_s, rv_s, of_s):
    n = pl.program_id(1)
    tile = x_ref.shape[1]
    n_chunks = tile // CHUNK
    gw = GROUP_WIDTH

    @pl.when(n == 0)
    def _():
        e0[0:POOL_HIST, :] = jnp.zeros((POOL_HIST, gw), F32)
        state[...] = jnp.zeros_like(state)

    x = x_ref[0]
    ms = jnp.mean(x * x, axis=-1, keepdims=True)
    h = (x * lax.rsqrt(ms + NORM_EPS) * preg_ref[...]).astype(BF16)

    lane_head = _lane_head()

    pb = _dot(h, win_ref[:, 4 * gw:7 * gw])
    pc = _dot(h, win_ref[:, 7 * gw:9 * gw])
    pd = _dot(h, win_ref[:, 9 * gw:13 * gw])

    bu = pb[:, 0:gw]
    bv = pb[:, gw:2 * gw]
    bg = pb[:, 2 * gw:3 * gw]
    mu = jnp.mean(bv, axis=-1, keepdims=True)
    dv = bv - mu
    var = jnp.mean(dv * dv, axis=-1, keepdims=True)
    vn_s[...] = (dv * lax.rsqrt(var + NORM_EPS) * sgug_ref[...]).astype(BF16)
    causal = (lax.broadcasted_iota(jnp.int32, (CHUNK, CHUNK), 0)
              >= lax.broadcasted_iota(jnp.int32, (CHUNK, CHUNK), 1))
    w_cat = jnp.concatenate(
        [jnp.where(causal, sguw_ref[g], 0.0).astype(BF16) for g in range(N_HEADS)], axis=1)
    for c in range(n_chunks):
        vc = vn_s[c * CHUNK:(c + 1) * CHUNK, :]
        v_stack = jnp.concatenate(
            [jnp.where(lane_head == g, vc, jnp.zeros_like(vc)) for g in range(N_HEADS)], axis=0)
        mix_s[c * CHUNK:(c + 1) * CHUNK, :] = sgub_ref[...] + _dot(w_cat, v_stack)
    y_ref[0, :, 0:gw] = (bu * mix_s[...] * _silu(bg)).astype(BF16)

    cx = pc[:, 0:gw]
    cg = pc[:, gw:2 * gw]
    hist = POOL_HIST
    e0[hist:hist + tile, :] = cx
    s2[8:hist + tile, :] = e0[8:hist + tile, :] + e0[7:hist + tile - 1, :]
    s4[16:hist + tile, :] = s2[16:hist + tile, :] + s2[14:hist + tile - 2, :]
    s8[24:hist + tile, :] = s4[24:hist + tile, :] + s4[20:hist + tile - 4, :]
    w16 = s8[hist:hist + tile, :] + s8[hist - 8:hist + tile - 8, :]
    lane = lax.broadcasted_iota(jnp.int32, (1, gw), 1)
    win = jnp.where(lane < 64, s2[hist:hist + tile, :],
                    jnp.where(lane < 128, s4[hist:hist + tile, :],
                              jnp.where(lane < 192, s8[hist:hist + tile, :], w16)))
    p_lane = jnp.where(lane < 64, POOL_SIZES[0],
                       jnp.where(lane < 128, POOL_SIZES[1],
                                 jnp.where(lane < 192, POOL_SIZES[2], POOL_SIZES[3])))
    t_glob = n * tile + lax.broadcasted_iota(jnp.int32, (tile, 1), 0)
    cnt = jnp.minimum(t_glob + 1, p_lane).astype(F32)
    pooled = win / cnt - cx
    yc = _dot(pooled.astype(BF16), poolw_ref[...]) * pools_ref[...]
    y_ref[0, :, gw:2 * gw] = (yc * _silu(cg)).astype(BF16)
    e0[0:hist, :] = e0[tile:tile + hist, :]

    rq_s[...] = pd[:, 0:gw]
    rk_s[...] = pd[:, gw:2 * gw] * (HEAD_DIM ** -0.5)
    rv_s[...] = pd[:, 2 * gw:3 * gw]
    dg = pd[:, 3 * gw:4 * gw]
    row_head = lax.broadcasted_iota(jnp.int32, (gw, 1), 0) // HEAD_DIM
    same_head = row_head == lane_head
    seg = jnp.where(same_head, 1.0 / HEAD_DIM, 0.0).astype(BF16)

    st = state[...]
    for c in range(n_chunks):
        rs = slice(c * CHUNK, (c + 1) * CHUNK)
        qc = rq_s[rs, :]
        kc = rk_s[rs, :]
        vb = rv_s[rs, :].astype(BF16)
        kb = kc.astype(BF16)
        k_stack = jnp.concatenate(
            [jnp.where(lane_head == hh, kb, jnp.zeros_like(kb)) for hh in range(N_HEADS)], axis=0)
        v_stack = jnp.concatenate(
            [jnp.where(lane_head == hh, vb, jnp.zeros_like(vb)) for hh in range(N_HEADS)], axis=0)
        sc = _dot_nt(qc.astype(BF16), k_stack) * decay_ref[...]
        of = _dot(sc.astype(BF16), v_stack)
        of_s[rs, :] = of + _dot((qc * xi_ref[...]).astype(BF16), st.astype(BF16))
        kv = _dot_tn((kc * zeta_ref[...]).astype(BF16), vb)
        st = st * cdec_ref[...] + jnp.where(same_head, kv, 0.0)
    state[...] = st
    of = of_s[...]
    mu_h = _seg_mean(of, seg)
    dh = of - mu_h
    var_h = _dot((dh * dh).astype(BF16), seg)
    yd = dh * lax.rsqrt(var_h + NORM_EPS) * retg_ref[...]
    y_ref[0, :, 2 * gw:3 * gw] = (yd * _silu(dg)).astype(BF16)

    pa = _dot(h, win_ref[:, 0:4 * gw])
    qa = pa[:, 0:gw] * (HEAD_DIM ** -0.5 * LOG2E)
    ka = pa[:, gw:2 * gw]
    va = pa[:, 2 * gw:3 * gw]
    for s in range(N_SLABS):
        q_ref[0, s] = qa[:, s * LANE:(s + 1) * LANE]
        k_ref[0, s] = ka[:, s * LANE:(s + 1) * LANE]
        v_ref[0, s] = va[:, s * LANE:(s + 1) * LANE]
    ga_ref[0] = _silu(pa[:, 3 * gw:4 * gw]).astype(BF16)


def _layer_spec(layer, shape):
    return pl.BlockSpec((None,) + tuple(shape), lambda bi, ni: (layer,) + (0,) * len(shape))


def _inproj_mix(layer, x, pre_g, w_in, sgu_g, sgu_w, sgu_bias, pool_w_bd, pool_scale, ret_g,
                decay, zeta, xi, cdec):
    b, s, d = x.shape
    tile = IN_TILE
    gw = GROUP_WIDTH
    const2 = lambda bi, ni: (0, 0)
    slab_spec = pl.BlockSpec((1, N_SLABS, tile, LANE), lambda bi, ni: (bi, 0, ni, 0))
    slab_shape = jax.ShapeDtypeStruct((b, N_SLABS, s, LANE), F32)
    return pl.pallas_call(
        _inproj_mix_kernel,
        grid=(b, s // tile),
        in_specs=[
            pl.BlockSpec((1, tile, d), lambda bi, ni: (bi, ni, 0)),
            _layer_spec(layer, (1, d)),
            _layer_spec(layer, w_in.shape[1:]),
            _layer_spec(layer, (1, gw)),
            _layer_spec(layer, sgu_w.shape[1:]),
            _layer_spec(layer, (CHUNK, gw)),
            _layer_spec(layer, (gw, gw)),
            _layer_spec(layer, (1, gw)),
            _layer_spec(layer, (1, gw)),
            pl.BlockSpec(decay.shape, const2),
            pl.BlockSpec((CHUNK, gw), const2),
            pl.BlockSpec((CHUNK, gw), const2),
            pl.BlockSpec((1, gw), const2),
        ],
        out_specs=[
            slab_spec, slab_spec, slab_spec,
            pl.BlockSpec((1, tile, gw), lambda bi, ni: (bi, ni, 0)),
            pl.BlockSpec((1, tile, 3 * gw), lambda bi, ni: (bi, ni, 0)),
        ],
        out_shape=[
            slab_shape, slab_shape, slab_shape,
            jax.ShapeDtypeStruct((b, s, gw), BF16),
            jax.ShapeDtypeStruct((b, s, 3 * gw), BF16),
        ],
        scratch_shapes=[
            pltpu.VMEM((tile + POOL_HIST, gw), F32),
            pltpu.VMEM((tile + POOL_HIST, gw), F32),
            pltpu.VMEM((tile + POOL_HIST, gw), F32),
            pltpu.VMEM((tile + POOL_HIST, gw), F32),
            pltpu.VMEM((gw, gw), F32),
            pltpu.VMEM((tile, gw), BF16),
            pltpu.VMEM((tile, gw), F32),
            pltpu.VMEM((tile, gw), F32),
            pltpu.VMEM((tile, gw), F32),
            pltpu.VMEM((tile, gw), F32),
            pltpu.VMEM((tile, gw), F32),
        ],
        compiler_params=pltpu.CompilerParams(
            dimension_semantics=("arbitrary", "arbitrary"), vmem_limit_bytes=VMEM_LIMIT),
        name="inproj_mix",
    )(x, pre_g, w_in, sgu_g, sgu_w, sgu_bias, pool_w_bd, pool_scale, ret_g, decay, zeta, xi, cdec)


def _dilated_attn_kernel(q_ref, k_ref, v_ref, ga_ref, o_ref, bias_s, ob_s, lb_s, kc_s, vc_s):
    bi = pl.program_id(0)
    qb = pl.program_id(1)
    base = qb * ATT_TILE
    lane_head = _lane_head()

    @pl.when((bi == 0) & (qb == 0))
    def _():
        kc_s[...] = jnp.zeros_like(kc_s)
        vc_s[...] = jnp.zeros_like(vc_s)
        ki = lax.broadcasted_iota(jnp.int32, (2 * CHUNK, CHUNK), 0)
        qi = lax.broadcasted_iota(jnp.int32, (2 * CHUNK, CHUNK), 1)
        dist = qi + CHUNK - ki
        band = (dist >= 0) & (dist <= CHUNK)
        distf = dist.astype(F32)
        for di, dil in enumerate(DILATIONS):
            for hh in range(N_HEADS):
                slope = 2.0 ** (-8.0 * (hh + 1) / N_HEADS)
                bias = -(slope * dil * LOG2E) * distf
                cs = pl.ds(hh * CHUNK, CHUNK)
                bias_s[0, di, :, cs] = jnp.where(band, bias, -jnp.inf)
                bias_s[1, di, :, cs] = jnp.where(band & (ki >= CHUNK), bias, -jnp.inf)

    def rows(ref, start, stride, dtype=BF16):
        idx = pl.ds(start, CHUNK) if stride == 1 else pl.ds(start, CHUNK, stride=stride)
        return jnp.concatenate([ref[0, s, idx, :] for s in range(N_SLABS)], axis=-1).astype(dtype)

    def key_block(di, start):
        dil = DILATIONS[di]
        return rows(k_ref, start, dil), rows(v_ref, start, dil, F32).T.astype(BF16)

    def attn_scores(di, q_start, q_local, first, prev_keys):
        dil = DILATIONS[di]
        span = dil * CHUNK
        cur = q_start
        fi = first.astype(jnp.int32)
        if prev_keys is None:
            prev_keys = key_block(di, jnp.where(first, cur, cur - span))
        cur_keys = key_block(di, cur)
        qv = rows(q_ref, q_local, dil)
        kk = jnp.concatenate([prev_keys[0], cur_keys[0]], axis=0)
        v_t = jnp.concatenate([prev_keys[1], cur_keys[1]], axis=1)
        qs = jnp.concatenate(
            [jnp.where(lane_head == hh, qv, jnp.zeros_like(qv)) for hh in range(N_HEADS)], axis=0)
        sc = _dot_nt(kk, qs) + bias_s[fi, di]
        return (sc, v_t), cur_keys

    def attn_values(sc, v_t):
        m = jnp.max(sc, axis=0, keepdims=True)
        pb = jnp.exp2(sc - m).astype(BF16)
        ones = jnp.ones((ONES_ROWS, 2 * CHUNK), BF16)
        o_parts, lse_parts = [], []
        for hh in range(N_HEADS):
            cs = slice(hh * CHUNK, (hh + 1) * CHUNK)
            v_ext = jnp.concatenate([v_t[hh * HEAD_DIM:(hh + 1) * HEAD_DIM, :], ones], axis=0)
            oh = _dot(v_ext, pb[:, cs])
            l = oh[HEAD_DIM:HEAD_DIM + 1, :]
            o_parts.append(oh[0:HEAD_DIM, :] * (1.0 / l))
            ls = m[:, cs] + jnp.log2(l)
            lse_parts.append(jnp.broadcast_to(ls, (HEAD_DIM, CHUNK)))
        o_t = jnp.concatenate(o_parts, axis=0)
        lse_t = jnp.concatenate(lse_parts, axis=0)
        return o_t.T, lse_t.T

    def finish_strided(di, slot, q_local, staged):
        o, lse = attn_values(*staged)
        idx = pl.ds(q_local, CHUNK, stride=DILATIONS[di])
        for s in range(N_SLABS):
            ob_s[slot, s, idx, :] = o[:, s * LANE:(s + 1) * LANE]
            lb_s[slot, s, idx, :] = lse[:, s * LANE:(s + 1) * LANE]

    def finish_dense(di, slot, q_local, staged):
        o1, l1 = attn_values(*staged)
        idx = pl.ds(q_local, CHUNK)
        o4 = jnp.concatenate([ob_s[0, s, idx, :] for s in range(N_SLABS)], axis=-1)
        l4 = jnp.concatenate([lb_s[0, s, idx, :] for s in range(N_SLABS)], axis=-1)
        o16 = jnp.concatenate([ob_s[1, s, idx, :] for s in range(N_SLABS)], axis=-1)
        l16 = jnp.concatenate([lb_s[1, s, idx, :] for s in range(N_SLABS)], axis=-1)
        mx = jnp.maximum(jnp.maximum(l1, l4), l16)
        w1 = jnp.exp2(l1 - mx)
        w4 = jnp.exp2(l4 - mx)
        w16 = jnp.exp2(l16 - mx)
        o = (w1 * o1 + w4 * o4 + w16 * o16) / (w1 + w4 + w16)
        o_ref[0, idx, :] = (o * ga_ref[0, idx, :].astype(F32)).astype(BF16)

    def run_branch(di, slot, finish):
        span = DILATIONS[di] * CHUNK
        per_class = ATT_TILE // span

        def group(g, carry):
            in_flight = []
            keys = None
            for u in range(ATT_GROUP):
                i = g * ATT_GROUP + u
                nb = i % per_class
                q_local = i // per_class + nb * span
                if di == 0:
                    q_local = pl.multiple_of(q_local, CHUNK)
                first = (qb == 0) & (nb == 0)
                if per_class == 1:
                    keys = (kc_s[i], vc_s[i])
                elif u % min(per_class, ATT_GROUP) == 0:
                    keys = None
                staged, keys = attn_scores(di, base + q_local, q_local, first, keys)
                if per_class == 1:
                    kc_s[i], vc_s[i] = keys
                in_flight.append((q_local, staged))
                if len(in_flight) > ATT_LOOKAHEAD:
                    finish(di, slot, *in_flight.pop(0))
            for args in in_flight:
                finish(di, slot, *args)
            return carry

        lax.fori_loop(0, ATT_TILE // CHUNK // ATT_GROUP, group, 0)

    run_branch(1, 0, finish_strided)
    run_branch(2, 1, finish_strided)
    run_branch(0, None, finish_dense)


def _dilated_attn(q, k, v, ga):
    b, _, s, _ = q.shape
    gw = GROUP_WIDTH
    n_branch = len(DILATIONS)
    return pl.pallas_call(
        _dilated_attn_kernel,
        grid=(b, s // ATT_TILE),
        in_specs=[
            pl.BlockSpec((1, N_SLABS, ATT_TILE, LANE), lambda bi, qi: (bi, 0, qi, 0)),
            pl.BlockSpec((1, N_SLABS, s, LANE), lambda bi, qi: (bi, 0, 0, 0)),
            pl.BlockSpec((1, N_SLABS, s, LANE), lambda bi, qi: (bi, 0, 0, 0)),
            pl.BlockSpec((1, ATT_TILE, gw), lambda bi, qi: (bi, qi, 0)),
        ],
        out_specs=pl.BlockSpec((1, ATT_TILE, gw), lambda bi, qi: (bi, qi, 0)),
        out_shape=jax.ShapeDtypeStruct((b, s, gw), BF16),
        scratch_shapes=[
            pltpu.VMEM((2, n_branch, 2 * CHUNK, N_HEADS * CHUNK), F32),
            pltpu.VMEM((n_branch - 1, N_SLABS, ATT_TILE, LANE), F32),
            pltpu.VMEM((n_branch - 1, N_SLABS, ATT_TILE, LANE), F32),
            pltpu.VMEM((DILATIONS[-1], CHUNK, gw), BF16),
            pltpu.VMEM((DILATIONS[-1], gw, CHUNK), BF16),
        ],
        compiler_params=pltpu.CompilerParams(
            dimension_semantics=("arbitrary", "arbitrary"), vmem_limit_bytes=VMEM_LIMIT),
        name="dilated_attn",
    )(q, k, v, ga)


def _outproj_kernel(x_ref, ya_ref, y_ref, wout_ref, postg_ref, o_ref):
    gw = GROUP_WIDTH
    y = _dot(ya_ref[0], wout_ref[0:gw, :]) + _dot(y_ref[0], wout_ref[gw:, :])
    ms = jnp.mean(y * y, axis=-1, keepdims=True)
    o_ref[0] = x_ref[0] + y * lax.rsqrt(ms + NORM_EPS) * postg_ref[...]


def _outproj(layer, x, ya, ybcd, w_out, post_g):
    b, s, d = x.shape
    tile = OUT_TILE
    gw = GROUP_WIDTH
    return pl.pallas_call(
        _outproj_kernel,
        grid=(b, s // tile),
        in_specs=[
            pl.BlockSpec((1, tile, d), lambda bi, ni: (bi, ni, 0)),
            pl.BlockSpec((1, tile, gw), lambda bi, ni: (bi, ni, 0)),
            pl.BlockSpec((1, tile, 3 * gw), lambda bi, ni: (bi, ni, 0)),
            _layer_spec(layer, w_out.shape[1:]),
            _layer_spec(layer, (1, d)),
        ],
        out_specs=pl.BlockSpec((1, tile, d), lambda bi, ni: (bi, ni, 0)),
        out_shape=jax.ShapeDtypeStruct(x.shape, x.dtype),
        compiler_params=pltpu.CompilerParams(
            dimension_semantics=("arbitrary", "arbitrary"), vmem_limit_bytes=VMEM_LIMIT),
        name="outproj",
    )(x, ya, ybcd, w_out, post_g)


def _retention_tables():
    f32 = np.float32
    log_g = np.log(f32(1.0) - np.exp2(f32(-5.0) - np.arange(N_HEADS, dtype=f32)))
    i = np.arange(CHUNK, dtype=f32)
    diff = i[:, None] - i[None, :]
    decay = np.where(diff >= 0, np.exp(log_g[:, None, None] * np.maximum(diff, f32(0.0))), f32(0.0))
    zeta = np.exp(log_g[:, None] * (f32(CHUNK - 1) - i))
    xi = np.exp(log_g[:, None] * (i + f32(1.0)))
    cdec = np.exp(log_g * f32(CHUNK))
    per_lane = lambda t: np.repeat(t.T, HEAD_DIM, axis=1)
    decay = decay.transpose(1, 0, 2).reshape(CHUNK, N_HEADS * CHUNK)
    tables = decay, per_lane(zeta), per_lane(xi), np.repeat(cdec, HEAD_DIM)[None, :]
    return tuple(jnp.asarray(t, F32) for t in tables)


def kernel(x, pre_g, w_in, sgu_g, sgu_w, sgu_b, pool_w, pool_scale, ret_g, w_out, post_g):
    row = lambda p: p[:, None, :]
    sgu_bias = jnp.repeat(jnp.swapaxes(sgu_b, 1, 2), HEAD_DIM, axis=2)
    eye = jnp.eye(len(POOL_SIZES), dtype=pool_w.dtype)
    pool_w_bd = jnp.einsum('lgcd,gh->lgchd', pool_w, eye).reshape(
        pool_w.shape[0], GROUP_WIDTH, GROUP_WIDTH).astype(BF16)
    w_in_b, w_out_b = w_in.astype(BF16), w_out.astype(BF16)
    tables = _retention_tables()
    for layer in range(pre_g.shape[0]):
        q, k, v, ga, ybcd = _inproj_mix(
            layer, x, row(pre_g), w_in_b, row(sgu_g), sgu_w, sgu_bias, pool_w_bd, row(pool_scale),
            row(ret_g), *tables)
        ya = _dilated_attn(q, k, v, ga)
        x = _outproj(layer, x, ya, ybcd, w_out_b, row(post_g))
    return x
```

```python
import math

import jax
import numpy as np
import jax.numpy as jnp
from jax import lax
from jax.experimental import pallas as pl
from jax.experimental.pallas import tpu as pltpu

F32 = jnp.float32
BF16 = jnp.bfloat16

HEAD_DIM = 64
N_HEADS = 4
GROUP_WIDTH = HEAD_DIM * N_HEADS
LANE = 128
N_SLABS = GROUP_WIDTH // LANE
CHUNK = 128
DILATIONS = (1, 4, 16)
POOL_SIZES = (2, 4, 8, 16)
POOL_HIST = 32
NORM_EPS = 1e-6
LOG2E = math.log2(math.e)
IN_TILE = 1024
OUT_TILE = 2048
ATT_TILE = CHUNK * DILATIONS[-1]
ONES_ROWS = 16
ATT_GROUP = 16
ATT_LOOKAHEAD = 2
VMEM_LIMIT = 56 * 1024 * 1024


def _silu(x):
    return x / (1.0 + jnp.exp(-x))


def _lane_head(width=GROUP_WIDTH):
    return lax.broadcasted_iota(jnp.int32, (1, width), 1) // HEAD_DIM


def _dot(a, b):
    return jnp.dot(a, b, preferred_element_type=F32)


def _dot_nt(a, b):
    return lax.dot_general(a, b, (((1,), (1,)), ((), ())), preferred_element_type=F32)


def _dot_tn(a, b):
    return lax.dot_general(a, b, (((0,), (0,)), ((), ())), preferred_element_type=F32)


def _seg_mean(a, seg):
    hi = a.astype(BF16)
    lo = (a - hi.astype(F32)).astype(BF16)
    return _dot(hi, seg) + _dot(lo, seg)


def _inproj_mix_kernel(x_ref, preg_ref, win_ref, sgug_ref, sguw_ref, sgub_ref, poolw_ref, pools_ref,
                       retg_ref, decay_ref, zeta_ref, xi_ref, cdec_ref,
                       q_ref, k_ref, v_ref, ga_ref, y_ref,
                       e0, s2, s4, s8, state, vn_s, mix_s, rq_s, rk_s, rv_s, of_s):
    n = pl.program_id(1)
    tile = x_ref.shape[1]
    n_chunks = tile // CHUNK
    gw = GROUP_WIDTH

    @pl.when(n == 0)
    def _():
        e0[0:POOL_HIST, :] = jnp.zeros((POOL_HIST, gw), F32)
        state[...] = jnp.zeros_like(state)

    x = x_ref[0]
    ms = jnp.mean(x * x, axis=-1, keepdims=True)
    h = (x * lax.rsqrt(ms + NORM_EPS) * preg_ref[...]).astype(BF16)

    lane_head = _lane_head()

    pb = _dot(h, win_ref[:, 4 * gw:7 * gw])
    pc = _dot(h, win_ref[:, 7 * gw:9 * gw])
    pd = _dot(h, win_ref[:, 9 * gw:13 * gw])

    bu = pb[:, 0:gw]
    bv = pb[:, gw:2 * gw]
    bg = pb[:, 2 * gw:3 * gw]
    mu = jnp.mean(bv, axis=-1, keepdims=True)
    dv = bv - mu
    var = jnp.mean(dv * dv, axis=-1, keepdims=True)
    vn_s[...] = (dv * lax.rsqrt(var + NORM_EPS) * sgug_ref[...]).astype(BF16)
    causal = (lax.broadcasted_iota(jnp.int32, (CHUNK, CHUNK), 0)
              >= lax.broadcasted_iota(jnp.int32, (CHUNK, CHUNK), 1))
    w_cat = jnp.concatenate(
        [jnp.where(causal, sguw_ref[g], 0.0).astype(BF16) for g in range(N_HEADS)], axis=1)
    for c in range(n_chunks):
        vc = vn_s[c * CHUNK:(c + 1) * CHUNK, :]
        v_stack = jnp.concatenate(
            [jnp.where(lane_head == g, vc, jnp.zeros_like(vc)) for g in range(N_HEADS)], axis=0)
        mix_s[c * CHUNK:(c + 1) * CHUNK, :] = sgub_ref[...] + _dot(w_cat, v_stack)
    y_ref[0, :, 0:gw] = (bu * mix_s[...] * _silu(bg)).astype(BF16)

    cx = pc[:, 0:gw]
    cg = pc[:, gw:2 * gw]
    hist = POOL_HIST
    e0[hist:hist + tile, :] = cx
    s2[8:hist + tile, :] = e0[8:hist + tile, :] + e0[7:hist + tile - 1, :]
    s4[16:hist + tile, :] = s2[16:hist + tile, :] + s2[14:hist + tile - 2, :]
    s8[24:hist + tile, :] = s4[24:hist + tile, :] + s4[20:hist + tile - 4, :]
    w16 = s8[hist:hist + tile, :] + s8[hist - 8:hist + tile - 8, :]
    lane = lax.broadcasted_iota(jnp.int32, (1, gw), 1)
    win = jnp.where(lane < 64, s2[hist:hist + tile, :],
                    jnp.where(lane < 128, s4[hist:hist + tile, :],
                              jnp.where(lane < 192, s8[hist:hist + tile, :], w16)))
    p_lane = jnp.where(lane < 64, POOL_SIZES[0],
                       jnp.where(lane < 128, POOL_SIZES[1],
                                 jnp.where(lane < 192, POOL_SIZES[2], POOL_SIZES[3])))
    t_glob = n * tile + lax.broadcasted_iota(jnp.int32, (tile, 1), 0)
    cnt = jnp.minimum(t_glob + 1, p_lane).astype(F32)
    pooled = win / cnt - cx
    yc = _dot(pooled.astype(BF16), poolw_ref[...]) * pools_ref[...]
    y_ref[0, :, gw:2 * gw] = (yc * _silu(cg)).astype(BF16)
    e0[0:hist, :] = e0[tile:tile + hist, :]

    rq_s[...] = pd[:, 0:gw]
    rk_s[...] = pd[:, gw:2 * gw] * (HEAD_DIM ** -0.5)
    rv_s[...] = pd[:, 2 * gw:3 * gw]
    dg = pd[:, 3 * gw:4 * gw]
    row_head = lax.broadcasted_iota(jnp.int32, (gw, 1), 0) // HEAD_DIM
    same_head = row_head == lane_head
    seg = jnp.where(same_head, 1.0 / HEAD_DIM, 0.0).astype(BF16)

    st = state[...]
    for c in range(n_chunks):
        rs = slice(c * CHUNK, (c + 1) * CHUNK)
        qc = rq_s[rs, :]
        kc = rk_s[rs, :]
        vb = rv_s[rs, :].astype(BF16)
        kb = kc.astype(BF16)
        k_stack = jnp.concatenate(
            [jnp.where(lane_head == hh, kb, jnp.zeros_like(kb)) for hh in range(N_HEADS)], axis=0)
        v_stack = jnp.concatenate(
            [jnp.where(lane_head == hh, vb, jnp.zeros_like(vb)) for hh in range(N_HEADS)], axis=0)
        sc = _dot_nt(qc.astype(BF16), k_stack) * decay_ref[...]
        of = _dot(sc.astype(BF16), v_stack)
        of_s[rs, :] = of + _dot((qc * xi_ref[...]).astype(BF16), st.astype(BF16))
        kv = _dot_tn((kc * zeta_ref[...]).astype(BF16), vb)
        st = st * cdec_ref[...] + jnp.where(same_head, kv, 0.0)
    state[...] = st
    of = of_s[...]
    mu_h = _seg_mean(of, seg)
    dh = of - mu_h
    var_h = _dot((dh * dh).astype(BF16), seg)
    yd = dh * lax.rsqrt(var_h + NORM_EPS) * retg_ref[...]
    y_ref[0, :, 2 * gw:3 * gw] = (yd * _silu(dg)).astype(BF16)

    pa = _dot(h, win_ref[:, 0:4 * gw])
    qa = pa[:, 0:gw] * (HEAD_DIM ** -0.5 * LOG2E)
    ka = pa[:, gw:2 * gw]
    va = pa[:, 2 * gw:3 * gw]
    for s in range(N_SLABS):
        q_ref[0, s] = qa[:, s * LANE:(s + 1) * LANE]
        k_ref[0, s] = ka[:, s * LANE:(s + 1) * LANE]
        v_ref[0, s] = va[:, s * LANE:(s + 1) * LANE]
    ga_ref[0] = _silu(pa[:, 3 * gw:4 * gw]).astype(BF16)


def _layer_spec(layer, shape):
    return pl.BlockSpec((None,) + tuple(shape), lambda bi, ni: (layer,) + (0,) * len(shape))


def _inproj_mix(layer, x, pre_g, w_in, sgu_g, sgu_w, sgu_bias, pool_w_bd, pool_scale, ret_g,
                decay, zeta, xi, cdec):
    b, s, d = x.shape
    tile = IN_TILE
    gw = GROUP_WIDTH
    const2 = lambda bi, ni: (0, 0)
    slab_spec = pl.BlockSpec((1, N_SLABS, tile, LANE), lambda bi, ni: (bi, 0, ni, 0))
    slab_shape = jax.ShapeDtypeStruct((b, N_SLABS, s, LANE), F32)
    return pl.pallas_call(
        _inproj_mix_kernel,
        grid=(b, s // tile),
        in_specs=[
            pl.BlockSpec((1, tile, d), lambda bi, ni: (bi, ni, 0)),
            _layer_spec(layer, (1, d)),
            _layer_spec(layer, w_in.shape[1:]),
            _layer_spec(layer, (1, gw)),
            _layer_spec(layer, sgu_w.shape[1:]),
            _layer_spec(layer, (CHUNK, gw)),
            _layer_spec(layer, (gw, gw)),
            _layer_spec(layer, (1, gw)),
            _layer_spec(layer, (1, gw)),
            pl.BlockSpec(decay.shape, const2),
            pl.BlockSpec((CHUNK, gw), const2),
            pl.BlockSpec((CHUNK, gw), const2),
            pl.BlockSpec((1, gw), const2),
        ],
        out_specs=[
            slab_spec, slab_spec, slab_spec,
            pl.BlockSpec((1, tile, gw), lambda bi, ni: (bi, ni, 0)),
            pl.BlockSpec((1, tile, 3 * gw), lambda bi, ni: (bi, ni, 0)),
        ],
        out_shape=[
            slab_shape, slab_shape, slab_shape,
            jax.ShapeDtypeStruct((b, s, gw), BF16),
            jax.ShapeDtypeStruct((b, s, 3 * gw), BF16),
        ],
        scratch_shapes=[
            pltpu.VMEM((tile + POOL_HIST, gw), F32),
            pltpu.VMEM((tile + POOL_HIST, gw), F32),
            pltpu.VMEM((tile + POOL_HIST, gw), F32),
            pltpu.VMEM((tile + POOL_HIST, gw), F32),
            pltpu.VMEM((gw, gw), F32),
            pltpu.VMEM((tile, gw), BF16),
            pltpu.VMEM((tile, gw), F32),
            pltpu.VMEM((tile, gw), F32),
            pltpu.VMEM((tile, gw), F32),
            pltpu.VMEM((tile, gw), F32),
            pltpu.VMEM((tile, gw), F32),
        ],
        compiler_params=pltpu.CompilerParams(
            dimension_semantics=("arbitrary", "arbitrary"), vmem_limit_bytes=VMEM_LIMIT),
        name="inproj_mix",
    )(x, pre_g, w_in, sgu_g, sgu_w, sgu_bias, pool_w_bd, pool_scale, ret_g, decay, zeta, xi, cdec)


def _dilated_attn_kernel(q_ref, k_ref, v_ref, ga_ref, o_ref, bias_s, ob_s, lb_s, kc_s, vc_s):
    bi = pl.program_id(0)
    qb = pl.program_id(1)
    base = qb * ATT_TILE
    lane_head = _lane_head()

    @pl.when((bi == 0) & (qb == 0))
    def _():
        kc_s[...] = jnp.zeros_like(kc_s)
        vc_s[...] = jnp.zeros_like(vc_s)
        ki = lax.broadcasted_iota(jnp.int32, (2 * CHUNK, CHUNK), 0)
        qi = lax.broadcasted_iota(jnp.int32, (2 * CHUNK, CHUNK), 1)
        dist = qi + CHUNK - ki
        band = (dist >= 0) & (dist <= CHUNK)
        distf = dist.astype(F32)
        for di, dil in enumerate(DILATIONS):
            for hh in range(N_HEADS):
                slope = 2.0 ** (-8.0 * (hh + 1) / N_HEADS)
                bias = -(slope * dil * LOG2E) * distf
                cs = pl.ds(hh * CHUNK, CHUNK)
                bias_s[0, di, :, cs] = jnp.where(band, bias, -jnp.inf)
                bias_s[1, di, :, cs] = jnp.where(band & (ki >= CHUNK), bias, -jnp.inf)

    def rows(ref, start, stride, dtype=BF16):
        idx = pl.ds(start, CHUNK) if stride == 1 else pl.ds(start, CHUNK, stride=stride)
        return jnp.concatenate([ref[0, s, idx, :] for s in range(N_SLABS)], axis=-1).astype(dtype)

    def key_block(di, start):
        dil = DILATIONS[di]
        return rows(k_ref, start, dil), rows(v_ref, start, dil, F32).T.astype(BF16)

    def attn_scores(di, q_start, q_local, first, prev_keys):
        dil = DILATIONS[di]
        span = dil * CHUNK
        cur = q_start
        fi = first.astype(jnp.int32)
        if prev_keys is None:
            prev_keys = key_block(di, jnp.where(first, cur, cur - span))
        cur_keys = key_block(di, cur)
        qv = rows(q_ref, q_local, dil)
        kk = jnp.concatenate([prev_keys[0], cur_keys[0]], axis=0)
        v_t = jnp.concatenate([prev_keys[1], cur_keys[1]], axis=1)
        qs = jnp.concatenate(
            [jnp.where(lane_head == hh, qv, jnp.zeros_like(qv)) for hh in range(N_HEADS)], axis=0)
        sc = _dot_nt(kk, qs) + bias_s[fi, di]
        return (sc, v_t), cur_keys

    def attn_values(sc, v_t):
        m = jnp.max(sc, axis=0, keepdims=True)
        pb = jnp.exp2(sc - m).astype(BF16)
        ones = jnp.ones((ONES_ROWS, 2 * CHUNK), BF16)
        o_parts, lse_parts = [], []
        for hh in range(N_HEADS):
            cs = slice(hh * CHUNK, (hh + 1) * CHUNK)
            v_ext = jnp.concatenate([v_t[hh * HEAD_DIM:(hh + 1) * HEAD_DIM, :], ones], axis=0)
            oh = _dot(v_ext, pb[:, cs])
            l = oh[HEAD_DIM:HEAD_DIM + 1, :]
            o_parts.append(oh[0:HEAD_DIM, :] * (1.0 / l))
            ls = m[:, cs] + jnp.log2(l)
            lse_parts.append(jnp.broadcast_to(ls, (HEAD_DIM, CHUNK)))
        o_t = jnp.concatenate(o_parts, axis=0)
        lse_t = jnp.concatenate(lse_parts, axis=0)
        return o_t.T, lse_t.T

    def finish_strided(di, slot, q_local, staged):
        o, lse = attn_values(*staged)
        idx = pl.ds(q_local, CHUNK, stride=DILATIONS[di])
        for s in range(N_SLABS):
            ob_s[slot, s, idx, :] = o[:, s * LANE:(s + 1) * LANE]
            lb_s[slot, s, idx, :] = lse[:, s * LANE:(s + 1) * LANE]

    def finish_dense(di, slot, q_local, staged):
        o1, l1 = attn_values(*staged)
        idx = pl.ds(q_local, CHUNK)
        o4 = jnp.concatenate([ob_s[0, s, idx, :] for s in range(N_SLABS)], axis=-1)
        l4 = jnp.concatenate([lb_s[0, s, idx, :] for s in range(N_SLABS)], axis=-1)
        o16 = jnp.concatenate([ob_s[1, s, idx, :] for s in range(N_SLABS)], axis=-1)
        l16 = jnp.concatenate([lb_s[1, s, idx, :] for s in range(N_SLABS)], axis=-1)
        mx = jnp.maximum(jnp.maximum(l1, l4), l16)
        w1 = jnp.exp2(l1 - mx)
        w4 = jnp.exp2(l4 - mx)
        w16 = jnp.exp2(l16 - mx)
        o = (w1 * o1 + w4 * o4 + w16 * o16) / (w1 + w4 + w16)
        o_ref[0, idx, :] = (o * ga_ref[0, idx, :].astype(F32)).astype(BF16)

    def run_branch(di, slot, finish):
        span = DILATIONS[di] * CHUNK
        per_class = ATT_TILE // span

        def group(g, carry):
            in_flight = []
            keys = None
            for u in range(ATT_GROUP):
                i = g * ATT_GROUP + u
                nb = i % per_class
                q_local = i // per_class + nb * span
                if di == 0:
                    q_local = pl.multiple_of(q_local, CHUNK)
                first = (qb == 0) & (nb == 0)
                if per_class == 1:
                    keys = (kc_s[i], vc_s[i])
                elif u % min(per_class, ATT_GROUP) == 0:
                    keys = None
                staged, keys = attn_scores(di, base + q_local, q_local, first, keys)
                if per_class == 1:
                    kc_s[i], vc_s[i] = keys
                in_flight.append((q_local, staged))
                if len(in_flight) > ATT_LOOKAHEAD:
                    finish(di, slot, *in_flight.pop(0))
            for args in in_flight:
                finish(di, slot, *args)
            return carry

        lax.fori_loop(0, ATT_TILE // CHUNK // ATT_GROUP, group, 0)

    run_branch(1, 0, finish_strided)
    run_branch(2, 1, finish_strided)
    run_branch(0, None, finish_dense)


def _dilated_attn(q, k, v, ga):
    b, _, s, _ = q.shape
    gw = GROUP_WIDTH
    n_branch = len(DILATIONS)
    return pl.pallas_call(
        _dilated_attn_kernel,
        grid=(b, s // ATT_TILE),
        in_specs=[
            pl.BlockSpec((1, N_SLABS, ATT_TILE, LANE), lambda bi, qi: (bi, 0, qi, 0)),
            pl.BlockSpec((1, N_SLABS, s, LANE), lambda bi, qi: (bi, 0, 0, 0)),
            pl.BlockSpec((1, N_SLABS, s, LANE), lambda bi, qi: (bi, 0, 0, 0)),
            pl.BlockSpec((1, ATT_TILE, gw), lambda bi, qi: (bi, qi, 0)),
        ],
        out_specs=pl.BlockSpec((1, ATT_TILE, gw), lambda bi, qi: (bi, qi, 0)),
        out_shape=jax.ShapeDtypeStruct((b, s, gw), BF16),
        scratch_shapes=[
            pltpu.VMEM((2, n_branch, 2 * CHUNK, N_HEADS * CHUNK), F32),
            pltpu.VMEM((n_branch - 1, N_SLABS, ATT_TILE, LANE), F32),
            pltpu.VMEM((n_branch - 1, N_SLABS, ATT_TILE, LANE), F32),
            pltpu.VMEM((DILATIONS[-1], CHUNK, gw), BF16),
            pltpu.VMEM((DILATIONS[-1], gw, CHUNK), BF16),
        ],
        compiler_params=pltpu.CompilerParams(
            dimension_semantics=("arbitrary", "arbitrary"), vmem_limit_bytes=VMEM_LIMIT),
        name="dilated_attn",
    )(q, k, v, ga)


def _outproj_kernel(x_ref, ya_ref, y_ref, wout_ref, postg_ref, o_ref):
    gw = GROUP_WIDTH
    y = _dot(ya_ref[0], wout_ref[0:gw, :]) + _dot(y_ref[0], wout_ref[gw:, :])
    ms = jnp.mean(y * y, axis=-1, keepdims=True)
    o_ref[0] = x_ref[0] + y * lax.rsqrt(ms + NORM_EPS) * postg_ref[...]


def _outproj(layer, x, ya, ybcd, w_out, post_g):
    b, s, d = x.shape
    tile = OUT_TILE
    gw = GROUP_WIDTH
    return pl.pallas_call(
        _outproj_kernel,
        grid=(b, s // tile),
        in_specs=[
            pl.BlockSpec((1, tile, d), lambda bi, ni: (bi, ni, 0)),
            pl.BlockSpec((1, tile, gw), lambda bi, ni: (bi, ni, 0)),
            pl.BlockSpec((1, tile, 3 * gw), lambda bi, ni: (bi, ni, 0)),
            _layer_spec(layer, w_out.shape[1:]),
            _layer_spec(layer, (1, d)),
        ],
        out_specs=pl.BlockSpec((1, tile, d), lambda bi, ni: (bi, ni, 0)),
        out_shape=jax.ShapeDtypeStruct(x.shape, x.dtype),
        compiler_params=pltpu.CompilerParams(
            dimension_semantics=("arbitrary", "arbitrary"), vmem_limit_bytes=VMEM_LIMIT),
        name="outproj",
    )(x, ya, ybcd, w_out, post_g)


def _retention_tables():
    f32 = np.float32
    log_g = np.log(f32(1.0) - np.exp2(f32(-5.0) - np.arange(N_HEADS, dtype=f32)))
    i = np.arange(CHUNK, dtype=f32)
    diff = i[:, None] - i[None, :]
    decay = np.where(diff >= 0, np.exp(log_g[:, None, None] * np.maximum(diff, f32(0.0))), f32(0.0))
    zeta = np.exp(log_g[:, None] * (f32(CHUNK - 1) - i))
    xi = np.exp(log_g[:, None] * (i + f32(1.0)))
    cdec = np.exp(log_g * f32(CHUNK))
    per_lane = lambda t: np.repeat(t.T, HEAD_DIM, axis=1)
    decay = decay.transpose(1, 0, 2).reshape(CHUNK, N_HEADS * CHUNK)
    tables = decay, per_lane(zeta), per_lane(xi), np.repeat(cdec, HEAD_DIM)[None, :]
    return tuple(jnp.asarray(t, F32) for t in tables)


def kernel(x, pre_g, w_in, sgu_g, sgu_w, sgu_b, pool_w, pool_scale, ret_g, w_out, post_g):
    row = lambda p: p[:, None, :]
    sgu_bias = jnp.repeat(jnp.swapaxes(sgu_b, 1, 2), HEAD_DIM, axis=2)
    eye = jnp.eye(len(POOL_SIZES), dtype=pool_w.dtype)
    pool_w_bd = jnp.einsum('lgcd,gh->lgchd', pool_w, eye).reshape(
        pool_w.shape[0], GROUP_WIDTH, GROUP_WIDTH).astype(BF16)
    w_in_b, w_out_b = w_in.astype(BF16), w_out.astype(BF16)
    tables = _retention_tables()
    for layer in range(pre_g.shape[0]):
        q, k, v, ga, ybcd = _inproj_mix(
            layer, x, row(pre_g), w_in_b, row(sgu_g), sgu_w, sgu_bias, pool_w_bd, row(pool_scale),
            row(ret_g), *tables)
        ya = _dilated_attn(q, k, v, ga)
        x = _outproj(layer, x, ya, ybcd, w_out_b, row(post_g))
    return x
```

```python
import math

import jax
import numpy as np
import jax.numpy as jnp
from jax import lax
from jax.experimental import pallas as pl
from jax.experimental.pallas import tpu as pltpu

F32 = jnp.float32
BF16 = jnp.bfloat16

HEAD_DIM = 64
N_HEADS = 4
GROUP_WIDTH = HEAD_DIM * N_HEADS
LANE = 128
N_SLABS = GROUP_WIDTH // LANE
CHUNK = 128
DILATIONS = (1, 4, 16)
POOL_SIZES = (2, 4, 8, 16)
POOL_HIST = 32
NORM_EPS = 1e-6
LOG2E = math.log2(math.e)
IN_TILE = 1024
OUT_TILE = 2048
ATT_TILE = CHUNK * DILATIONS[-1]
ONES_ROWS = 16
ATT_GROUP = 16
ATT_LOOKAHEAD = 2
VMEM_LIMIT = 56 * 1024 * 1024


def _silu(x):
    return x / (1.0 + jnp.exp(-x))


def _lane_head(width=GROUP_WIDTH):
    return lax.broadcasted_iota(jnp.int32, (1, width), 1) // HEAD_DIM


def _dot(a, b):
    return jnp.dot(a, b, preferred_element_type=F32)


def _dot_nt(a, b):
    return lax.dot_general(a, b, (((1,), (1,)), ((), ())), preferred_element_type=F32)


def _dot_tn(a, b):
    return lax.dot_general(a, b, (((0,), (0,)), ((), ())), preferred_element_type=F32)


def _seg_mean(a, seg):
    hi = a.astype(BF16)
    lo = (a - hi.astype(F32)).astype(BF16)
    return _dot(hi, seg) + _dot(lo, seg)


def _inproj_mix_kernel(x_ref, preg_ref, win_ref, sgug_ref, sguw_ref, sgub_ref, poolw_ref, pools_ref,
                       retg_ref, decay_ref, zeta_ref, xi_ref, cdec_ref,
                       q_ref, k_ref, v_ref, ga_ref, y_ref,
                       e0, s2, s4, s8, state, vn_s, mix_s, rq_s, rk_s, rv_s, of_s):
    n = pl.program_id(1)
    tile = x_ref.shape[1]
    n_chunks = tile // CHUNK
    gw = GROUP_WIDTH

    @pl.when(n == 0)
    def _():
        e0[0:POOL_HIST, :] = jnp.zeros((POOL_HIST, gw), F32)
        state[...] = jnp.zeros_like(state)

    x = x_ref[0]
    ms = jnp.mean(x * x, axis=-1, keepdims=True)
    h = (x * lax.rsqrt(ms + NORM_EPS) * preg_ref[...]).astype(BF16)

    lane_head = _lane_head()

    pb = _dot(h, win_ref[:, 4 * gw:7 * gw])
    pc = _dot(h, win_ref[:, 7 * gw:9 * gw])
    pd = _dot(h, win_ref[:, 9 * gw:13 * gw])

    bu = pb[:, 0:gw]
    bv = pb[:, gw:2 * gw]
    bg = pb[:, 2 * gw:3 * gw]
    mu = jnp.mean(bv, axis=-1, keepdims=True)
    dv = bv - mu
    var = jnp.mean(dv * dv, axis=-1, keepdims=True)
    vn_s[...] = (dv * lax.rsqrt(var + NORM_EPS) * sgug_ref[...]).astype(BF16)
    causal = (lax.broadcasted_iota(jnp.int32, (CHUNK, CHUNK), 0)
              >= lax.broadcasted_iota(jnp.int32, (CHUNK, CHUNK), 1))
    w_cat = jnp.concatenate(
        [jnp.where(causal, sguw_ref[g], 0.0).astype(BF16) for g in range(N_HEADS)], axis=1)
    for c in range(n_chunks):
        vc = vn_s[c * CHUNK:(c + 1) * CHUNK, :]
        v_stack = jnp.concatenate(
            [jnp.where(lane_head == g, vc, jnp.zeros_like(vc)) for g in range(N_HEADS)], axis=0)
        mix_s[c * CHUNK:(c + 1) * CHUNK, :] = sgub_ref[...] + _dot(w_cat, v_stack)
    y_ref[0, :, 0:gw] = (bu * mix_s[...] * _silu(bg)).astype(BF16)

    cx = pc[:, 0:gw]
    cg = pc[:, gw:2 * gw]
    hist = POOL_HIST
    e0[hist:hist + tile, :] = cx
    s2[8:hist + tile, :] = e0[8:hist + tile, :] + e0[7:hist + tile - 1, :]
    s4[16:hist + tile, :] = s2[16:hist + tile, :] + s2[14:hist + tile - 2, :]
    s8[24:hist + tile, :] = s4[24:hist + tile, :] + s4[20:hist + tile - 4, :]
    w16 = s8[hist:hist + tile, :] + s8[hist - 8:hist + tile - 8, :]
    lane = lax.broadcasted_iota(jnp.int32, (1, gw), 1)
    win = jnp.where(lane < 64, s2[hist:hist + tile, :],
                    jnp.where(lane < 128, s4[hist:hist + tile, :],
                              jnp.where(lane < 192, s8[hist:hist + tile, :], w16)))
    p_lane = jnp.where(lane < 64, POOL_SIZES[0],
                       jnp.where(lane < 128, POOL_SIZES[1],
                                 jnp.where(lane < 192, POOL_SIZES[2], POOL_SIZES[3])))
    t_glob = n * tile + lax.broadcasted_iota(jnp.int32, (tile, 1), 0)
    cnt = jnp.minimum(t_glob + 1, p_lane).astype(F32)
    pooled = win / cnt - cx
    yc = _dot(pooled.astype(BF16), poolw_ref[...]) * pools_ref[...]
    y_ref[0, :, gw:2 * gw] = (yc * _silu(cg)).astype(BF16)
    e0[0:hist, :] = e0[tile:tile + hist, :]

    rq_s[...] = pd[:, 0:gw]
    rk_s[...] = pd[:, gw:2 * gw] * (HEAD_DIM ** -0.5)
    rv_s[...] = pd[:, 2 * gw:3 * gw]
    dg = pd[:, 3 * gw:4 * gw]
    row_head = lax.broadcasted_iota(jnp.int32, (gw, 1), 0) // HEAD_DIM
    same_head = row_head == lane_head
    seg = jnp.where(same_head, 1.0 / HEAD_DIM, 0.0).astype(BF16)

    st = state[...]
    for c in range(n_chunks):
        rs = slice(c * CHUNK, (c + 1) * CHUNK)
        qc = rq_s[rs, :]
        kc = rk_s[rs, :]
        vb = rv_s[rs, :].astype(BF16)
        kb = kc.astype(BF16)
        k_stack = jnp.concatenate(
            [jnp.where(lane_head == hh, kb, jnp.zeros_like(kb)) for hh in range(N_HEADS)], axis=0)
        v_stack = jnp.concatenate(
            [jnp.where(lane_head == hh, vb, jnp.zeros_like(vb)) for hh in range(N_HEADS)], axis=0)
        sc = _dot_nt(qc.astype(BF16), k_stack) * decay_ref[...]
        of = _dot(sc.astype(BF16), v_stack)
        of_s[rs, :] = of + _dot((qc * xi_ref[...]).astype(BF16), st.astype(BF16))
        kv = _dot_tn((kc * zeta_ref[...]).astype(BF16), vb)
        st = st * cdec_ref[...] + jnp.where(same_head, kv, 0.0)
    state[...] = st
    of = of_s[...]
    mu_h = _seg_mean(of, seg)
    dh = of - mu_h
    var_h = _dot((dh * dh).astype(BF16), seg)
    yd = dh * lax.rsqrt(var_h + NORM_EPS) * retg_ref[...]
    y_ref[0, :, 2 * gw:3 * gw] = (yd * _silu(dg)).astype(BF16)

    pa = _dot(h, win_ref[:, 0:4 * gw])
    qa = pa[:, 0:gw] * (HEAD_DIM ** -0.5 * LOG2E)
    ka = pa[:, gw:2 * gw]
    va = pa[:, 2 * gw:3 * gw]
    for s in range(N_SLABS):
        q_ref[0, s] = qa[:, s * LANE:(s + 1) * LANE]
        k_ref[0, s] = ka[:, s * LANE:(s + 1) * LANE]
        v_ref[0, s] = va[:, s * LANE:(s + 1) * LANE]
    ga_ref[0] = _silu(pa[:, 3 * gw:4 * gw]).astype(BF16)


def _layer_spec(layer, shape):
    return pl.BlockSpec((None,) + tuple(shape), lambda bi, ni: (layer,) + (0,) * len(shape))


def _inproj_mix(layer, x, pre_g, w_in, sgu_g, sgu_w, sgu_bias, pool_w_bd, pool_scale, ret_g,
                decay, zeta, xi, cdec):
    b, s, d = x.shape
    tile = IN_TILE
    gw = GROUP_WIDTH
    const2 = lambda bi, ni: (0, 0)
    slab_spec = pl.BlockSpec((1, N_SLABS, tile, LANE), lambda bi, ni: (bi, 0, ni, 0))
    slab_shape = jax.ShapeDtypeStruct((b, N_SLABS, s, LANE), F32)
    return pl.pallas_call(
        _inproj_mix_kernel,
        grid=(b, s // tile),
        in_specs=[
            pl.BlockSpec((1, tile, d), lambda bi, ni: (bi, ni, 0)),
            _layer_spec(layer, (1, d)),
            _layer_spec(layer, w_in.shape[1:]),
            _layer_spec(layer, (1, gw)),
            _layer_spec(layer, sgu_w.shape[1:]),
            _layer_spec(layer, (CHUNK, gw)),
            _layer_spec(layer, (gw, gw)),
            _layer_spec(layer, (1, gw)),
            _layer_spec(layer, (1, gw)),
            pl.BlockSpec(decay.shape, const2),
            pl.BlockSpec((CHUNK, gw), const2),
            pl.BlockSpec((CHUNK, gw), const2),
            pl.BlockSpec((1, gw), const2),
        ],
        out_specs=[
            slab_spec, slab_spec, slab_spec,
            pl.BlockSpec((1, tile, gw), lambda bi, ni: (bi, ni, 0)),
            pl.BlockSpec((1, tile, 3 * gw), lambda bi, ni: (bi, ni, 0)),
        ],
        out_shape=[
            slab_shape, slab_shape, slab_shape,
            jax.ShapeDtypeStruct((b, s, gw), BF16),
            jax.ShapeDtypeStruct((b, s, 3 * gw), BF16),
        ],
        scratch_shapes=[
            pltpu.VMEM((tile + POOL_HIST, gw), F32),
            pltpu.VMEM((tile + POOL_HIST, gw), F32),
            pltpu.VMEM((tile + POOL_HIST, gw), F32),
            pltpu.VMEM((tile + POOL_HIST, gw), F32),
            pltpu.VMEM((gw, gw), F32),
            pltpu.VMEM((tile, gw), BF16),
            pltpu.VMEM((tile, gw), F32),
            pltpu.VMEM((tile, gw), F32),
            pltpu.VMEM((tile, gw), F32),
            pltpu.VMEM((tile, gw), F32),
            pltpu.VMEM((tile, gw), F32),
        ],
        compiler_params=pltpu.CompilerParams(
            dimension_semantics=("arbitrary", "arbitrary"), vmem_limit_bytes=VMEM_LIMIT),
        name="inproj_mix",
    )(x, pre_g, w_in, sgu_g, sgu_w, sgu_bias, pool_w_bd, pool_scale, ret_g, decay, zeta, xi, cdec)


def _dilated_attn_kernel(q_ref, k_ref, v_ref, ga_ref, o_ref, bias_s, ob_s, lb_s, kc_s, vc_s):
    bi = pl.program_id(0)
    qb = pl.program_id(1)
    base = qb * ATT_TILE
    lane_head = _lane_head()

    @pl.when((bi == 0) & (qb == 0))
    def _():
        kc_s[...] = jnp.zeros_like(kc_s)
        vc_s[...] = jnp.zeros_like(vc_s)
        ki = lax.broadcasted_iota(jnp.int32, (2 * CHUNK, CHUNK), 0)
        qi = lax.broadcasted_iota(jnp.int32, (2 * CHUNK, CHUNK), 1)
        dist = qi + CHUNK - ki
        band = (dist >= 0) & (dist <= CHUNK)
        distf = dist.astype(F32)
        for di, dil in enumerate(DILATIONS):
            for hh in range(N_HEADS):
                slope = 2.0 ** (-8.0 * (hh + 1) / N_HEADS)
                bias = -(slope * dil * LOG2E) * distf
                cs = pl.ds(hh * CHUNK, CHUNK)
                bias_s[0, di, :, cs] = jnp.where(band, bias, -jnp.inf)
                bias_s[1, di, :, cs] = jnp.where(band & (ki >= CHUNK), bias, -jnp.inf)

    def rows(ref, start, stride):
        idx = pl.ds(start, CHUNK) if stride == 1 else pl.ds(start, CHUNK, stride=stride)
        return jnp.concatenate([ref[0, s, idx, :] for s in range(N_SLABS)], axis=-1).astype(BF16)

    def key_block(di, start):
        dil = DILATIONS[di]
        return rows(k_ref, start, dil), rows(v_ref, start, dil).T

    def attn_scores(di, q_start, q_local, first, prev_keys):
        dil = DILATIONS[di]
        span = dil * CHUNK
        cur = q_start
        fi = first.astype(jnp.int32)
        if prev_keys is None:
            prev_keys = key_block(di, jnp.where(first, cur, cur - span))
        cur_keys = key_block(di, cur)
        qv = rows(q_ref, q_local, dil)
        kk = jnp.concatenate([prev_keys[0], cur_keys[0]], axis=0)
        v_t = jnp.concatenate([prev_keys[1], cur_keys[1]], axis=1)
        qs = jnp.concatenate(
            [jnp.where(lane_head == hh, qv, jnp.zeros_like(qv)) for hh in range(N_HEADS)], axis=0)
        sc = _dot_nt(kk, qs) + bias_s[fi, di]
        return (sc, v_t), cur_keys

    def attn_values(sc, v_t):
        m = jnp.max(sc, axis=0, keepdims=True)
        pb = jnp.exp2(sc - m).astype(BF16)
        ones = jnp.ones((ONES_ROWS, 2 * CHUNK), BF16)
        o_parts, lse_parts = [], []
        for hh in range(N_HEADS):
            cs = slice(hh * CHUNK, (hh + 1) * CHUNK)
            v_ext = jnp.concatenate([v_t[hh * HEAD_DIM:(hh + 1) * HEAD_DIM, :], ones], axis=0)
            oh = _dot(v_ext, pb[:, cs])
            l = oh[HEAD_DIM:HEAD_DIM + 1, :]
            o_parts.append(oh[0:HEAD_DIM, :] * (1.0 / l))
            ls = m[:, cs] + jnp.log2(l)
            lse_parts.append(jnp.broadcast_to(ls, (HEAD_DIM, CHUNK)))
        o_t = jnp.concatenate(o_parts, axis=0)
        lse_t = jnp.concatenate(lse_parts, axis=0)
        return o_t.T, lse_t.T

    def finish_strided(di, slot, q_local, staged):
        o, lse = attn_values(*staged)
        idx = pl.ds(q_local, CHUNK, stride=DILATIONS[di])
        for s in range(N_SLABS):
            ob_s[slot, s, idx, :] = o[:, s * LANE:(s + 1) * LANE]
            lb_s[slot, s, idx, :] = lse[:, s * LANE:(s + 1) * LANE]

    def finish_dense(di, slot, q_local, staged):
        o1, l1 = attn_values(*staged)
        idx = pl.ds(q_local, CHUNK)
        o4 = jnp.concatenate([ob_s[0, s, idx, :] for s in range(N_SLABS)], axis=-1)
        l4 = jnp.concatenate([lb_s[0, s, idx, :] for s in range(N_SLABS)], axis=-1)
        o16 = jnp.concatenate([ob_s[1, s, idx, :] for s in range(N_SLABS)], axis=-1)
        l16 = jnp.concatenate([lb_s[1, s, idx, :] for s in range(N_SLABS)], axis=-1)
        mx = jnp.maximum(jnp.maximum(l1, l4), l16)
        w1 = jnp.exp2(l1 - mx)
        w4 = jnp.exp2(l4 - mx)
        w16 = jnp.exp2(l16 - mx)
        o = (w1 * o1 + w4 * o4 + w16 * o16) / (w1 + w4 + w16)
        o_ref[0, idx, :] = (o * ga_ref[0, idx, :].astype(F32)).astype(BF16)

    def run_branch(di, slot, finish):
        span = DILATIONS[di] * CHUNK
        per_class = ATT_TILE // span

        def group(g, carry):
            in_flight = []
            keys = None
            for u in range(ATT_GROUP):
                i = g * ATT_GROUP + u
                nb = i % per_class
                q_local = i // per_class + nb * span
                if di == 0:
                    q_local = pl.multiple_of(q_local, CHUNK)
                first = (qb == 0) & (nb == 0)
                if per_class == 1:
                    keys = (kc_s[i], vc_s[i])
                elif u % min(per_class, ATT_GROUP) == 0:
                    keys = None
                staged, keys = attn_scores(di, base + q_local, q_local, first, keys)
                if per_class == 1:
                    kc_s[i], vc_s[i] = keys
                in_flight.append((q_local, staged))
                if len(in_flight) > ATT_LOOKAHEAD:
                    finish(di, slot, *in_flight.pop(0))
            for args in in_flight:
                finish(di, slot, *args)
            return carry

        lax.fori_loop(0, ATT_TILE // CHUNK // ATT_GROUP, group, 0)

    run_branch(1, 0, finish_strided)
    run_branch(2, 1, finish_strided)
    run_branch(0, None, finish_dense)


def _dilated_attn(q, k, v, ga):
    b, _, s, _ = q.shape
    gw = GROUP_WIDTH
    n_branch = len(DILATIONS)
    return pl.pallas_call(
        _dilated_attn_kernel,
        grid=(b, s // ATT_TILE),
        in_specs=[
            pl.BlockSpec((1, N_SLABS, ATT_TILE, LANE), lambda bi, qi: (bi, 0, qi, 0)),
            pl.BlockSpec((1, N_SLABS, s, LANE), lambda bi, qi: (bi, 0, 0, 0)),
            pl.BlockSpec((1, N_SLABS, s, LANE), lambda bi, qi: (bi, 0, 0, 0)),
            pl.BlockSpec((1, ATT_TILE, gw), lambda bi, qi: (bi, qi, 0)),
        ],
        out_specs=pl.BlockSpec((1, ATT_TILE, gw), lambda bi, qi: (bi, qi, 0)),
        out_shape=jax.ShapeDtypeStruct((b, s, gw), BF16),
        scratch_shapes=[
            pltpu.VMEM((2, n_branch, 2 * CHUNK, N_HEADS * CHUNK), F32),
            pltpu.VMEM((n_branch - 1, N_SLABS, ATT_TILE, LANE), F32),
            pltpu.VMEM((n_branch - 1, N_SLABS, ATT_TILE, LANE), F32),
            pltpu.VMEM((DILATIONS[-1], CHUNK, gw), BF16),
            pltpu.VMEM((DILATIONS[-1], gw, CHUNK), BF16),
        ],
        compiler_params=pltpu.CompilerParams(
            dimension_semantics=("arbitrary", "arbitrary"), vmem_limit_bytes=VMEM_LIMIT),
        name="dilated_attn",
    )(q, k, v, ga)


def _outproj_kernel(x_ref, ya_ref, y_ref, wout_ref, postg_ref, o_ref):
    gw = GROUP_WIDTH
    y = _dot(ya_ref[0], wout_ref[0:gw, :]) + _dot(y_ref[0], wout_ref[gw:, :])
    ms = jnp.mean(y * y, axis=-1, keepdims=True)
    o_ref[0] = x_ref[0] + y * lax.rsqrt(ms + NORM_EPS) * postg_ref[...]


def _outproj(layer, x, ya, ybcd, w_out, post_g):
    b, s, d = x.shape
    tile = OUT_TILE
    gw = GROUP_WIDTH
    return pl.pallas_call(
        _outproj_kernel,
        grid=(b, s // tile),
        in_specs=[
            pl.BlockSpec((1, tile, d), lambda bi, ni: (bi, ni, 0)),
            pl.BlockSpec((1, tile, gw), lambda bi, ni: (bi, ni, 0)),
            pl.BlockSpec((1, tile, 3 * gw), lambda bi, ni: (bi, ni, 0)),
            _layer_spec(layer, w_out.shape[1:]),
            _layer_spec(layer, (1, d)),
        ],
        out_specs=pl.BlockSpec((1, tile, d), lambda bi, ni: (bi, ni, 0)),
        out_shape=jax.ShapeDtypeStruct(x.shape, x.dtype),
        compiler_params=pltpu.CompilerParams(
            dimension_semantics=("arbitrary", "arbitrary"), vmem_limit_bytes=VMEM_LIMIT),
        name="outproj",
    )(x, ya, ybcd, w_out, post_g)


def _retention_tables():
    f32 = np.float32
    log_g = np.log(f32(1.0) - np.exp2(f32(-5.0) - np.arange(N_HEADS, dtype=f32)))
    i = np.arange(CHUNK, dtype=f32)
    diff = i[:, None] - i[None, :]
    decay = np.where(diff >= 0, np.exp(log_g[:, None, None] * np.maximum(diff, f32(0.0))), f32(0.0))
    zeta = np.exp(log_g[:, None] * (f32(CHUNK - 1) - i))
    xi = np.exp(log_g[:, None] * (i + f32(1.0)))
    cdec = np.exp(log_g * f32(CHUNK))
    per_lane = lambda t: np.repeat(t.T, HEAD_DIM, axis=1)
    decay = decay.transpose(1, 0, 2).reshape(CHUNK, N_HEADS * CHUNK)
    tables = decay, per_lane(zeta), per_lane(xi), np.repeat(cdec, HEAD_DIM)[None, :]
    return tuple(jnp.asarray(t, F32) for t in tables)


def kernel(x, pre_g, w_in, sgu_g, sgu_w, sgu_b, pool_w, pool_scale, ret_g, w_out, post_g):
    row = lambda p: p[:, None, :]
    sgu_bias = jnp.repeat(jnp.swapaxes(sgu_b, 1, 2), HEAD_DIM, axis=2)
    eye = jnp.eye(len(POOL_SIZES), dtype=pool_w.dtype)
    pool_w_bd = jnp.einsum('lgcd,gh->lgchd', pool_w, eye).reshape(
        pool_w.shape[0], GROUP_WIDTH, GROUP_WIDTH).astype(BF16)
    w_in_b, w_out_b = w_in.astype(BF16), w_out.astype(BF16)
    tables = _retention_tables()
    for layer in range(pre_g.shape[0]):
        q, k, v, ga, ybcd = _inproj_mix(
            layer, x, row(pre_g), w_in_b, row(sgu_g), sgu_w, sgu_bias, pool_w_bd, row(pool_scale),
            row(ret_g), *tables)
        ya = _dilated_attn(q, k, v, ga)
        x = _outproj(layer, x, ya, ybcd, w_out_b, row(post_g))
    return x
```

```python
import math

import jax
import numpy as np
import jax.numpy as jnp
from jax import lax
from jax.experimental import pallas as pl
from jax.experimental.pallas import tpu as pltpu

F32 = jnp.float32
BF16 = jnp.bfloat16

HEAD_DIM = 64
N_HEADS = 4
GROUP_WIDTH = HEAD_DIM * N_HEADS
LANE = 128
N_SLABS = GROUP_WIDTH // LANE
CHUNK = 128
DILATIONS = (1, 4, 16)
POOL_SIZES = (2, 4, 8, 16)
POOL_HIST = 32
NORM_EPS = 1e-6
LOG2E = math.log2(math.e)
IN_TILE = 1024
OUT_TILE = 2048
ATT_TILE = CHUNK * DILATIONS[-1]
ONES_ROWS = 16
ATT_GROUP = 16
ATT_LOOKAHEAD = 2
VMEM_LIMIT = 56 * 1024 * 1024


def _silu(x):
    return x / (1.0 + jnp.exp(-x))


def _lane_head(width=GROUP_WIDTH):
    return lax.broadcasted_iota(jnp.int32, (1, width), 1) // HEAD_DIM


def _dot(a, b):
    return jnp.dot(a, b, preferred_element_type=F32)


def _dot_nt(a, b):
    return lax.dot_general(a, b, (((1,), (1,)), ((), ())), preferred_element_type=F32)


def _dot_tn(a, b):
    return lax.dot_general(a, b, (((0,), (0,)), ((), ())), preferred_element_type=F32)


def _seg_mean(a, seg):
    hi = a.astype(BF16)
    lo = (a - hi.astype(F32)).astype(BF16)
    return _dot(hi, seg) + _dot(lo, seg)


def _inproj_mix_kernel(x_ref, preg_ref, win_ref, sgug_ref, sguw_ref, sgub_ref, poolw_ref, pools_ref,
                       retg_ref, decay_ref, zeta_ref, xi_ref, cdec_ref,
                       q_ref, k_ref, v_ref, ga_ref, y_ref,
                       e0, s2, s4, s8, state, vn_s, mix_s, rq_s, rk_s, rv_s, of_s, wbf_s):
    n = pl.program_id(1)
    tile = x_ref.shape[1]
    n_chunks = tile // CHUNK
    gw = GROUP_WIDTH

    @pl.when((pl.program_id(0) == 0) & (n == 0))
    def _():
        wbf_s[...] = win_ref[...].astype(BF16)

    @pl.when(n == 0)
    def _():
        e0[0:POOL_HIST, :] = jnp.zeros((POOL_HIST, gw), F32)
        state[...] = jnp.zeros_like(state)

    x = x_ref[0]
    ms = jnp.mean(x * x, axis=-1, keepdims=True)
    h = (x * lax.rsqrt(ms + NORM_EPS) * preg_ref[...]).astype(BF16)

    lane_head = _lane_head()

    pb = _dot(h, wbf_s[:, 4 * gw:7 * gw])
    pc = _dot(h, wbf_s[:, 7 * gw:9 * gw])
    pd = _dot(h, wbf_s[:, 9 * gw:13 * gw])

    bu = pb[:, 0:gw]
    bv = pb[:, gw:2 * gw]
    bg = pb[:, 2 * gw:3 * gw]
    mu = jnp.mean(bv, axis=-1, keepdims=True)
    dv = bv - mu
    var = jnp.mean(dv * dv, axis=-1, keepdims=True)
    vn_s[...] = (dv * lax.rsqrt(var + NORM_EPS) * sgug_ref[...]).astype(BF16)
    causal = (lax.broadcasted_iota(jnp.int32, (CHUNK, CHUNK), 0)
              >= lax.broadcasted_iota(jnp.int32, (CHUNK, CHUNK), 1))
    w_cat = jnp.concatenate(
        [jnp.where(causal, sguw_ref[g], 0.0).astype(BF16) for g in range(N_HEADS)], axis=1)
    for c in range(n_chunks):
        vc = vn_s[c * CHUNK:(c + 1) * CHUNK, :]
        v_stack = jnp.concatenate(
            [jnp.where(lane_head == g, vc, jnp.zeros_like(vc)) for g in range(N_HEADS)], axis=0)
        mix_s[c * CHUNK:(c + 1) * CHUNK, :] = sgub_ref[...] + _dot(w_cat, v_stack)
    y_ref[0, :, 0:gw] = (bu * mix_s[...] * _silu(bg)).astype(BF16)

    cx = pc[:, 0:gw]
    cg = pc[:, gw:2 * gw]
    hist = POOL_HIST
    e0[hist:hist + tile, :] = cx
    s2[8:hist + tile, :] = e0[8:hist + tile, :] + e0[7:hist + tile - 1, :]
    s4[16:hist + tile, :] = s2[16:hist + tile, :] + s2[14:hist + tile - 2, :]
    s8[24:hist + tile, :] = s4[24:hist + tile, :] + s4[20:hist + tile - 4, :]
    w16 = s8[hist:hist + tile, :] + s8[hist - 8:hist + tile - 8, :]
    lane = lax.broadcasted_iota(jnp.int32, (1, gw), 1)
    win = jnp.where(lane < 64, s2[hist:hist + tile, :],
                    jnp.where(lane < 128, s4[hist:hist + tile, :],
                              jnp.where(lane < 192, s8[hist:hist + tile, :], w16)))
    p_lane = jnp.where(lane < 64, POOL_SIZES[0],
                       jnp.where(lane < 128, POOL_SIZES[1],
                                 jnp.where(lane < 192, POOL_SIZES[2], POOL_SIZES[3])))
    t_glob = n * tile + lax.broadcasted_iota(jnp.int32, (tile, 1), 0)
    cnt = jnp.minimum(t_glob + 1, p_lane).astype(F32)
    pooled = win / cnt - cx
    yc = _dot(pooled.astype(BF16), poolw_ref[...]) * pools_ref[...]
    y_ref[0, :, gw:2 * gw] = (yc * _silu(cg)).astype(BF16)
    e0[0:hist, :] = e0[tile:tile + hist, :]

    rq_s[...] = pd[:, 0:gw]
    rk_s[...] = pd[:, gw:2 * gw] * (HEAD_DIM ** -0.5)
    rv_s[...] = pd[:, 2 * gw:3 * gw]
    dg = pd[:, 3 * gw:4 * gw]
    row_head = lax.broadcasted_iota(jnp.int32, (gw, 1), 0) // HEAD_DIM
    same_head = row_head == lane_head
    seg = jnp.where(same_head, 1.0 / HEAD_DIM, 0.0).astype(BF16)

    st = state[...]
    for c in range(n_chunks):
        rs = slice(c * CHUNK, (c + 1) * CHUNK)
        qc = rq_s[rs, :]
        kc = rk_s[rs, :]
        vb = rv_s[rs, :].astype(BF16)
        kb = kc.astype(BF16)
        k_stack = jnp.concatenate(
            [jnp.where(lane_head == hh, kb, jnp.zeros_like(kb)) for hh in range(N_HEADS)], axis=0)
        v_stack = jnp.concatenate(
            [jnp.where(lane_head == hh, vb, jnp.zeros_like(vb)) for hh in range(N_HEADS)], axis=0)
        sc = _dot_nt(qc.astype(BF16), k_stack) * decay_ref[...]
        of = _dot(sc.astype(BF16), v_stack)
        of_s[rs, :] = of + _dot((qc * xi_ref[...]).astype(BF16), st.astype(BF16))
        kv = _dot_tn((kc * zeta_ref[...]).astype(BF16), vb)
        st = st * cdec_ref[...] + jnp.where(same_head, kv, 0.0)
    state[...] = st
    of = of_s[...]
    mu_h = _seg_mean(of, seg)
    dh = of - mu_h
    var_h = _dot((dh * dh).astype(BF16), seg)
    yd = dh * lax.rsqrt(var_h + NORM_EPS) * retg_ref[...]
    y_ref[0, :, 2 * gw:3 * gw] = (yd * _silu(dg)).astype(BF16)

    pa = _dot(h, wbf_s[:, 0:4 * gw])
    qa = pa[:, 0:gw] * (HEAD_DIM ** -0.5 * LOG2E)
    ka = pa[:, gw:2 * gw]
    va = pa[:, 2 * gw:3 * gw]
    for s in range(N_SLABS):
        q_ref[0, s] = qa[:, s * LANE:(s + 1) * LANE]
        k_ref[0, s] = ka[:, s * LANE:(s + 1) * LANE]
        v_ref[0, s] = va[:, s * LANE:(s + 1) * LANE]
    ga_ref[0] = _silu(pa[:, 3 * gw:4 * gw]).astype(BF16)


def _layer_spec(layer, shape):
    return pl.BlockSpec((None,) + tuple(shape), lambda bi, ni: (layer,) + (0,) * len(shape))


def _inproj_mix(layer, x, pre_g, w_in, sgu_g, sgu_w, sgu_bias, pool_w_bd, pool_scale, ret_g,
                decay, zeta, xi, cdec):
    b, s, d = x.shape
    tile = IN_TILE
    gw = GROUP_WIDTH
    const2 = lambda bi, ni: (0, 0)
    slab_spec = pl.BlockSpec((1, N_SLABS, tile, LANE), lambda bi, ni: (bi, 0, ni, 0))
    slab_shape = jax.ShapeDtypeStruct((b, N_SLABS, s, LANE), F32)
    return pl.pallas_call(
        _inproj_mix_kernel,
        grid=(b, s // tile),
        in_specs=[
            pl.BlockSpec((1, tile, d), lambda bi, ni: (bi, ni, 0)),
            _layer_spec(layer, (1, d)),
            _layer_spec(layer, w_in.shape[1:]),
            _layer_spec(layer, (1, gw)),
            _layer_spec(layer, sgu_w.shape[1:]),
            _layer_spec(layer, (CHUNK, gw)),
            _layer_spec(layer, (gw, gw)),
            _layer_spec(layer, (1, gw)),
            _layer_spec(layer, (1, gw)),
            pl.BlockSpec(decay.shape, const2),
            pl.BlockSpec((CHUNK, gw), const2),
            pl.BlockSpec((CHUNK, gw), const2),
            pl.BlockSpec((1, gw), const2),
        ],
        out_specs=[
            slab_spec, slab_spec, slab_spec,
            pl.BlockSpec((1, tile, gw), lambda bi, ni: (bi, ni, 0)),
            pl.BlockSpec((1, tile, 3 * gw), lambda bi, ni: (bi, ni, 0)),
        ],
        out_shape=[
            slab_shape, slab_shape, slab_shape,
            jax.ShapeDtypeStruct((b, s, gw), BF16),
            jax.ShapeDtypeStruct((b, s, 3 * gw), BF16),
        ],
        scratch_shapes=[
            pltpu.VMEM((tile + POOL_HIST, gw), F32),
            pltpu.VMEM((tile + POOL_HIST, gw), F32),
            pltpu.VMEM((tile + POOL_HIST, gw), F32),
            pltpu.VMEM((tile + POOL_HIST, gw), F32),
            pltpu.VMEM((gw, gw), F32),
            pltpu.VMEM((tile, gw), BF16),
            pltpu.VMEM((tile, gw), F32),
            pltpu.VMEM((tile, gw), F32),
            pltpu.VMEM((tile, gw), F32),
            pltpu.VMEM((tile, gw), F32),
            pltpu.VMEM((tile, gw), F32),
            pltpu.VMEM(w_in.shape[1:], BF16),
        ],
        compiler_params=pltpu.CompilerParams(
            dimension_semantics=("arbitrary", "arbitrary"), vmem_limit_bytes=VMEM_LIMIT),
        name="inproj_mix",
    )(x, pre_g, w_in, sgu_g, sgu_w, sgu_bias, pool_w_bd, pool_scale, ret_g, decay, zeta, xi, cdec)


def _dilated_attn_kernel(q_ref, k_ref, v_ref, ga_ref, o_ref, bias_s, ob_s, lb_s, kc_s, vc_s):
    bi = pl.program_id(0)
    qb = pl.program_id(1)
    base = qb * ATT_TILE
    lane_head = _lane_head()

    @pl.when((bi == 0) & (qb == 0))
    def _():
        kc_s[...] = jnp.zeros_like(kc_s)
        vc_s[...] = jnp.zeros_like(vc_s)
        ki = lax.broadcasted_iota(jnp.int32, (2 * CHUNK, CHUNK), 0)
        qi = lax.broadcasted_iota(jnp.int32, (2 * CHUNK, CHUNK), 1)
        dist = qi + CHUNK - ki
        band = (dist >= 0) & (dist <= CHUNK)
        distf = dist.astype(F32)
        for di, dil in enumerate(DILATIONS):
            for hh in range(N_HEADS):
                slope = 2.0 ** (-8.0 * (hh + 1) / N_HEADS)
                bias = -(slope * dil * LOG2E) * distf
                cs = pl.ds(hh * CHUNK, CHUNK)
                bias_s[0, di, :, cs] = jnp.where(band, bias, -jnp.inf)
                bias_s[1, di, :, cs] = jnp.where(band & (ki >= CHUNK), bias, -jnp.inf)

    def rows(ref, start, stride):
        idx = pl.ds(start, CHUNK) if stride == 1 else pl.ds(start, CHUNK, stride=stride)
        return jnp.concatenate([ref[0, s, idx, :] for s in range(N_SLABS)], axis=-1).astype(BF16)

    def key_block(di, start):
        dil = DILATIONS[di]
        return rows(k_ref, start, dil), rows(v_ref, start, dil).T

    def attn_scores(di, q_start, q_local, first, prev_keys):
        dil = DILATIONS[di]
        span = dil * CHUNK
        cur = q_start
        fi = first.astype(jnp.int32)
        if prev_keys is None:
            prev_keys = key_block(di, jnp.where(first, cur, cur - span))
        cur_keys = key_block(di, cur)
        qv = rows(q_ref, q_local, dil)
        kk = jnp.concatenate([prev_keys[0], cur_keys[0]], axis=0)
        v_t = jnp.concatenate([prev_keys[1], cur_keys[1]], axis=1)
        qs = jnp.concatenate(
            [jnp.where(lane_head == hh, qv, jnp.zeros_like(qv)) for hh in range(N_HEADS)], axis=0)
        sc = _dot_nt(kk, qs) + bias_s[fi, di]
        return (sc, v_t), cur_keys

    def attn_values(sc, v_t):
        m = jnp.max(sc, axis=0, keepdims=True)
        pb = jnp.exp2(sc - m).astype(BF16)
        ones = jnp.ones((ONES_ROWS, 2 * CHUNK), BF16)
        o_parts, lse_parts = [], []
        for hh in range(N_HEADS):
            cs = slice(hh * CHUNK, (hh + 1) * CHUNK)
            v_ext = jnp.concatenate([v_t[hh * HEAD_DIM:(hh + 1) * HEAD_DIM, :], ones], axis=0)
            oh = _dot(v_ext, pb[:, cs])
            l = oh[HEAD_DIM:HEAD_DIM + 1, :]
            o_parts.append(oh[0:HEAD_DIM, :] * (1.0 / l))
            ls = m[:, cs] + jnp.log2(l)
            lse_parts.append(jnp.broadcast_to(ls, (HEAD_DIM, CHUNK)))
        o_t = jnp.concatenate(o_parts, axis=0)
        lse_t = jnp.concatenate(lse_parts, axis=0)
        return o_t.T, lse_t.T

    def finish_strided(di, slot, q_local, staged):
        o, lse = attn_values(*staged)
        idx = pl.ds(q_local, CHUNK, stride=DILATIONS[di])
        for s in range(N_SLABS):
            ob_s[slot, s, idx, :] = o[:, s * LANE:(s + 1) * LANE]
            lb_s[slot, s, idx, :] = lse[:, s * LANE:(s + 1) * LANE]

    def finish_dense(di, slot, q_local, staged):
        o1, l1 = attn_values(*staged)
        idx = pl.ds(q_local, CHUNK)
        o4 = jnp.concatenate([ob_s[0, s, idx, :] for s in range(N_SLABS)], axis=-1)
        l4 = jnp.concatenate([lb_s[0, s, idx, :] for s in range(N_SLABS)], axis=-1)
        o16 = jnp.concatenate([ob_s[1, s, idx, :] for s in range(N_SLABS)], axis=-1)
        l16 = jnp.concatenate([lb_s[1, s, idx, :] for s in range(N_SLABS)], axis=-1)
        mx = jnp.maximum(jnp.maximum(l1, l4), l16)
        w1 = jnp.exp2(l1 - mx)
        w4 = jnp.exp2(l4 - mx)
        w16 = jnp.exp2(l16 - mx)
        o = (w1 * o1 + w4 * o4 + w16 * o16) / (w1 + w4 + w16)
        o_ref[0, idx, :] = (o * ga_ref[0, idx, :].astype(F32)).astype(BF16)

    def run_branch(di, slot, finish):
        span = DILATIONS[di] * CHUNK
        per_class = ATT_TILE // span

        def group(g, carry):
            in_flight = []
            keys = None
            for u in range(ATT_GROUP):
                i = g * ATT_GROUP + u
                nb = i % per_class
                q_local = i // per_class + nb * span
                if di == 0:
                    q_local = pl.multiple_of(q_local, CHUNK)
                first = (qb == 0) & (nb == 0)
                if per_class == 1:
                    keys = (kc_s[i], vc_s[i])
                elif u % min(per_class, ATT_GROUP) == 0:
                    keys = None
                staged, keys = attn_scores(di, base + q_local, q_local, first, keys)
                if per_class == 1:
                    kc_s[i], vc_s[i] = keys
                in_flight.append((q_local, staged))
                if len(in_flight) > ATT_LOOKAHEAD:
                    finish(di, slot, *in_flight.pop(0))
            for args in in_flight:
                finish(di, slot, *args)
            return carry

        lax.fori_loop(0, ATT_TILE // CHUNK // ATT_GROUP, group, 0)

    run_branch(1, 0, finish_strided)
    run_branch(2, 1, finish_strided)
    run_branch(0, None, finish_dense)


def _dilated_attn(q, k, v, ga):
    b, _, s, _ = q.shape
    gw = GROUP_WIDTH
    n_branch = len(DILATIONS)
    return pl.pallas_call(
        _dilated_attn_kernel,
        grid=(b, s // ATT_TILE),
        in_specs=[
            pl.BlockSpec((1, N_SLABS, ATT_TILE, LANE), lambda bi, qi: (bi, 0, qi, 0)),
            pl.BlockSpec((1, N_SLABS, s, LANE), lambda bi, qi: (bi, 0, 0, 0)),
            pl.BlockSpec((1, N_SLABS, s, LANE), lambda bi, qi: (bi, 0, 0, 0)),
            pl.BlockSpec((1, ATT_TILE, gw), lambda bi, qi: (bi, qi, 0)),
        ],
        out_specs=pl.BlockSpec((1, ATT_TILE, gw), lambda bi, qi: (bi, qi, 0)),
        out_shape=jax.ShapeDtypeStruct((b, s, gw), BF16),
        scratch_shapes=[
            pltpu.VMEM((2, n_branch, 2 * CHUNK, N_HEADS * CHUNK), F32),
            pltpu.VMEM((n_branch - 1, N_SLABS, ATT_TILE, LANE), F32),
            pltpu.VMEM((n_branch - 1, N_SLABS, ATT_TILE, LANE), F32),
            pltpu.VMEM((DILATIONS[-1], CHUNK, gw), BF16),
            pltpu.VMEM((DILATIONS[-1], gw, CHUNK), BF16),
        ],
        compiler_params=pltpu.CompilerParams(
            dimension_semantics=("arbitrary", "arbitrary"), vmem_limit_bytes=VMEM_LIMIT),
        name="dilated_attn",
    )(q, k, v, ga)


def _outproj_kernel(x_ref, ya_ref, y_ref, wout_ref, postg_ref, o_ref):
    gw = GROUP_WIDTH
    w = wout_ref[...].astype(BF16)
    y = _dot(ya_ref[0], w[0:gw, :]) + _dot(y_ref[0], w[gw:, :])
    ms = jnp.mean(y * y, axis=-1, keepdims=True)
    o_ref[0] = x_ref[0] + y * lax.rsqrt(ms + NORM_EPS) * postg_ref[...]


def _outproj(layer, x, ya, ybcd, w_out, post_g):
    b, s, d = x.shape
    tile = OUT_TILE
    gw = GROUP_WIDTH
    return pl.pallas_call(
        _outproj_kernel,
        grid=(b, s // tile),
        in_specs=[
            pl.BlockSpec((1, tile, d), lambda bi, ni: (bi, ni, 0)),
            pl.BlockSpec((1, tile, gw), lambda bi, ni: (bi, ni, 0)),
            pl.BlockSpec((1, tile, 3 * gw), lambda bi, ni: (bi, ni, 0)),
            _layer_spec(layer, w_out.shape[1:]),
            _layer_spec(layer, (1, d)),
        ],
        out_specs=pl.BlockSpec((1, tile, d), lambda bi, ni: (bi, ni, 0)),
        out_shape=jax.ShapeDtypeStruct(x.shape, x.dtype),
        compiler_params=pltpu.CompilerParams(
            dimension_semantics=("arbitrary", "arbitrary"), vmem_limit_bytes=VMEM_LIMIT),
        name="outproj",
    )(x, ya, ybcd, w_out, post_g)


def _retention_tables():
    f32 = np.float32
    log_g = np.log(f32(1.0) - np.exp2(f32(-5.0) - np.arange(N_HEADS, dtype=f32)))
    i = np.arange(CHUNK, dtype=f32)
    diff = i[:, None] - i[None, :]
    decay = np.where(diff >= 0, np.exp(log_g[:, None, None] * np.maximum(diff, f32(0.0))), f32(0.0))
    zeta = np.exp(log_g[:, None] * (f32(CHUNK - 1) - i))
    xi = np.exp(log_g[:, None] * (i + f32(1.0)))
    cdec = np.exp(log_g * f32(CHUNK))
    per_lane = lambda t: np.repeat(t.T, HEAD_DIM, axis=1)
    decay = decay.transpose(1, 0, 2).reshape(CHUNK, N_HEADS * CHUNK)
    tables = decay, per_lane(zeta), per_lane(xi), np.repeat(cdec, HEAD_DIM)[None, :]
    return tuple(jnp.asarray(t, F32) for t in tables)


def kernel(x, pre_g, w_in, sgu_g, sgu_w, sgu_b, pool_w, pool_scale, ret_g, w_out, post_g):
    row = lambda p: p[:, None, :]
    sgu_bias = jnp.repeat(jnp.swapaxes(sgu_b, 1, 2), HEAD_DIM, axis=2)
    eye = jnp.eye(len(POOL_SIZES), dtype=pool_w.dtype)
    pool_w_bd = jnp.einsum('lgcd,gh->lgchd', pool_w, eye).reshape(
        pool_w.shape[0], GROUP_WIDTH, GROUP_WIDTH).astype(BF16)
    tables = _retention_tables()
    for layer in range(pre_g.shape[0]):
        q, k, v, ga, ybcd = _inproj_mix(
            layer, x, row(pre_g), w_in, row(sgu_g), sgu_w, sgu_bias, pool_w_bd, row(pool_scale),
            row(ret_g), *tables)
        ya = _dilated_attn(q, k, v, ga)
        x = _outproj(layer, x, ya, ybcd, w_out, row(post_g))
    return x
```

```python
import functools
import math

import jax
import numpy as np
import jax.numpy as jnp
from jax import lax
from jax.experimental import pallas as pl
from jax.experimental.pallas import tpu as pltpu

F32 = jnp.float32
BF16 = jnp.bfloat16

HEAD_DIM = 64
N_HEADS = 4
GROUP_WIDTH = HEAD_DIM * N_HEADS
LANE = 128
N_SLABS = GROUP_WIDTH // LANE
CHUNK = 128
DILATIONS = (1, 4, 16)
POOL_SIZES = (2, 4, 8, 16)
POOL_CH = GROUP_WIDTH // len(POOL_SIZES)
POOL_HIST = 32
NORM_EPS = 1e-6
LOG2E = math.log2(math.e)
IN_TILE = 1024
OUT_TILE = 2048
ATT_TILE = CHUNK * DILATIONS[-1]
ONES_ROWS = 16
ATT_GROUP = 16
ATT_LOOKAHEAD = 2
VMEM_LIMIT = 56 * 1024 * 1024


def _silu(x):
    return x / (1.0 + jnp.exp(-x))


def _lane_head(width=GROUP_WIDTH):
    return lax.broadcasted_iota(jnp.int32, (1, width), 1) // HEAD_DIM


def _dot(a, b):
    return jnp.dot(a, b, preferred_element_type=F32)


def _dot_nt(a, b):
    return lax.dot_general(a, b, (((1,), (1,)), ((), ())), preferred_element_type=F32)


def _dot_tn(a, b):
    return lax.dot_general(a, b, (((0,), (0,)), ((), ())), preferred_element_type=F32)


def _seg_mean(a, seg):
    hi = a.astype(BF16)
    lo = (a - hi.astype(F32)).astype(BF16)
    return _dot(hi, seg) + _dot(lo, seg)


def _inproj_mix_kernel(layer, x_ref, preg_ref, win_ref, sgug_ref, sguw_ref, sgub_ref, poolw_ref,
                       pools_ref, retg_ref, decay_ref, zeta_ref, xi_ref, cdec_ref,
                       q_ref, k_ref, v_ref, ga_ref, y_ref,
                       e0, s2, s4, s8, state, wbf_s):
    n = pl.program_id(1)
    tile = x_ref.shape[1]
    n_chunks = tile // CHUNK
    gw = GROUP_WIDTH

    @pl.when((pl.program_id(0) == 0) & (n == 0))
    def _():
        wbf_s[...] = win_ref[...].astype(BF16)

    @pl.when(n == 0)
    def _():
        e0[0:POOL_HIST, :] = jnp.zeros((POOL_HIST, gw), F32)
        state[...] = jnp.zeros_like(state)

    x = x_ref[0]
    ms = jnp.mean(x * x, axis=-1, keepdims=True)
    h = (x * lax.rsqrt(ms + NORM_EPS) * preg_ref[layer:layer + 1, :]).astype(BF16)

    lane_head = _lane_head()

    pb = _dot(h, wbf_s[:, 4 * gw:7 * gw])
    pc = _dot(h, wbf_s[:, 7 * gw:9 * gw])
    pd = _dot(h, wbf_s[:, 9 * gw:13 * gw])

    bu = pb[:, 0:gw]
    bv = pb[:, gw:2 * gw]
    bg = pb[:, 2 * gw:3 * gw]
    mu = jnp.mean(bv, axis=-1, keepdims=True)
    dv = bv - mu
    var = jnp.mean(dv * dv, axis=-1, keepdims=True)
    vn = (dv * lax.rsqrt(var + NORM_EPS) * sgug_ref[layer:layer + 1, :]).astype(BF16)
    causal = (lax.broadcasted_iota(jnp.int32, (CHUNK, CHUNK), 0)
              >= lax.broadcasted_iota(jnp.int32, (CHUNK, CHUNK), 1))
    w_cat = jnp.concatenate(
        [jnp.where(causal, sguw_ref[g], 0.0).astype(BF16) for g in range(N_HEADS)], axis=1)
    mixed = []
    for c in range(n_chunks):
        vc = vn[c * CHUNK:(c + 1) * CHUNK, :]
        v_stack = jnp.concatenate(
            [jnp.where(lane_head == g, vc, jnp.zeros_like(vc)) for g in range(N_HEADS)], axis=0)
        mixed.append(sgub_ref[...] + _dot(w_cat, v_stack))
    y_ref[0, :, 0:gw] = (bu * jnp.concatenate(mixed, axis=0) * _silu(bg)).astype(BF16)

    cx = pc[:, 0:gw]
    cg = pc[:, gw:2 * gw]
    hist = POOL_HIST
    e0[hist:hist + tile, :] = cx
    s2[8:hist + tile, :] = e0[8:hist + tile, :] + e0[7:hist + tile - 1, :]
    s4[16:hist + tile, :] = s2[16:hist + tile, :] + s2[14:hist + tile - 2, :]
    s8[24:hist + tile, :] = s4[24:hist + tile, :] + s4[20:hist + tile - 4, :]
    w16 = s8[hist:hist + tile, :] + s8[hist - 8:hist + tile - 8, :]
    pool_group = lax.broadcasted_iota(jnp.int32, (1, gw), 1) // POOL_CH
    win = jnp.where(pool_group == 0, s2[hist:hist + tile, :],
                    jnp.where(pool_group == 1, s4[hist:hist + tile, :],
                              jnp.where(pool_group == 2, s8[hist:hist + tile, :], w16)))
    p_lane = jnp.where(pool_group == 0, POOL_SIZES[0],
                       jnp.where(pool_group == 1, POOL_SIZES[1],
                                 jnp.where(pool_group == 2, POOL_SIZES[2], POOL_SIZES[3])))
    t_glob = n * tile + lax.broadcasted_iota(jnp.int32, (tile, 1), 0)
    cnt = jnp.minimum(t_glob + 1, p_lane).astype(F32)
    pooled = win / cnt - cx
    yc = _dot(pooled.astype(BF16), poolw_ref[...]) * pools_ref[layer:layer + 1, :]
    y_ref[0, :, gw:2 * gw] = (yc * _silu(cg)).astype(BF16)
    e0[0:hist, :] = e0[tile:tile + hist, :]

    dg = pd[:, 3 * gw:4 * gw]
    row_head = lax.broadcasted_iota(jnp.int32, (gw, 1), 0) // HEAD_DIM
    same_head = row_head == lane_head
    seg = jnp.where(same_head, 1.0 / HEAD_DIM, 0.0).astype(BF16)

    st = state[...]
    retained = []
    for c in range(n_chunks):
        rs = slice(c * CHUNK, (c + 1) * CHUNK)
        qc = pd[rs, 0:gw]
        kc = pd[rs, gw:2 * gw] * (HEAD_DIM ** -0.5)
        vb = pd[rs, 2 * gw:3 * gw].astype(BF16)
        kb = kc.astype(BF16)
        k_stack = jnp.concatenate(
            [jnp.where(lane_head == hh, kb, jnp.zeros_like(kb)) for hh in range(N_HEADS)], axis=0)
        v_stack = jnp.concatenate(
            [jnp.where(lane_head == hh, vb, jnp.zeros_like(vb)) for hh in range(N_HEADS)], axis=0)
        sc = _dot_nt(qc.astype(BF16), k_stack) * decay_ref[...]
        of = _dot(sc.astype(BF16), v_stack)
        retained.append(of + _dot((qc * xi_ref[...]).astype(BF16), st.astype(BF16)))
        kv = _dot_tn((kc * zeta_ref[...]).astype(BF16), vb)
        st = st * cdec_ref[...] + jnp.where(same_head, kv, 0.0)
    state[...] = st
    of = jnp.concatenate(retained, axis=0)
    mu_h = _seg_mean(of, seg)
    dh = of - mu_h
    var_h = _dot((dh * dh).astype(BF16), seg)
    yd = dh * lax.rsqrt(var_h + NORM_EPS) * retg_ref[layer:layer + 1, :]
    y_ref[0, :, 2 * gw:3 * gw] = (yd * _silu(dg)).astype(BF16)

    pa = _dot(h, wbf_s[:, 0:4 * gw])
    qa = pa[:, 0:gw] * (HEAD_DIM ** -0.5 * LOG2E)
    ka = pa[:, gw:2 * gw]
    va = pa[:, 2 * gw:3 * gw]
    for s in range(N_SLABS):
        q_ref[0, s] = qa[:, s * LANE:(s + 1) * LANE]
        k_ref[0, s] = ka[:, s * LANE:(s + 1) * LANE]
        v_ref[0, s] = va[:, s * LANE:(s + 1) * LANE]
    ga_ref[0] = _silu(pa[:, 3 * gw:4 * gw]).astype(BF16)


def _layer_spec(layer, shape):
    return pl.BlockSpec((None,) + tuple(shape), lambda bi, ni: (layer,) + (0,) * len(shape))


def _inproj_mix(layer, x, pre_g, w_in, sgu_g, sgu_w, sgu_bias, pool_w_bd, pool_scale, ret_g,
                decay, zeta, xi, cdec):
    b, s, d = x.shape
    tile = IN_TILE
    gw = GROUP_WIDTH
    const2 = lambda bi, ni: (0, 0)
    slab_spec = pl.BlockSpec((1, N_SLABS, tile, LANE), lambda bi, ni: (bi, 0, ni, 0))
    slab_shape = jax.ShapeDtypeStruct((b, N_SLABS, s, LANE), F32)
    return pl.pallas_call(
        functools.partial(_inproj_mix_kernel, layer),
        grid=(b, s // tile),
        in_specs=[
            pl.BlockSpec((1, tile, d), lambda bi, ni: (bi, ni, 0)),
            pl.BlockSpec(pre_g.shape, const2),
            _layer_spec(layer, w_in.shape[1:]),
            pl.BlockSpec(sgu_g.shape, const2),
            _layer_spec(layer, sgu_w.shape[1:]),
            _layer_spec(layer, (CHUNK, gw)),
            _layer_spec(layer, (gw, gw)),
            pl.BlockSpec(pool_scale.shape, const2),
            pl.BlockSpec(ret_g.shape, const2),
            pl.BlockSpec(decay.shape, const2),
            pl.BlockSpec((CHUNK, gw), const2),
            pl.BlockSpec((CHUNK, gw), const2),
            pl.BlockSpec((1, gw), const2),
        ],
        out_specs=[
            slab_spec, slab_spec, slab_spec,
            pl.BlockSpec((1, tile, gw), lambda bi, ni: (bi, ni, 0)),
            pl.BlockSpec((1, tile, 3 * gw), lambda bi, ni: (bi, ni, 0)),
        ],
        out_shape=[
            slab_shape, slab_shape, slab_shape,
            jax.ShapeDtypeStruct((b, s, gw), BF16),
            jax.ShapeDtypeStruct((b, s, 3 * gw), BF16),
        ],
        scratch_shapes=[
            pltpu.VMEM((tile + POOL_HIST, gw), F32),
            pltpu.VMEM((tile + POOL_HIST, gw), F32),
            pltpu.VMEM((tile + POOL_HIST, gw), F32),
            pltpu.VMEM((tile + POOL_HIST, gw), F32),
            pltpu.VMEM((gw, gw), F32),
            pltpu.VMEM(w_in.shape[1:], BF16),
        ],
        compiler_params=pltpu.CompilerParams(
            dimension_semantics=("arbitrary", "arbitrary"), vmem_limit_bytes=VMEM_LIMIT),
        name="inproj_mix",
    )(x, pre_g, w_in, sgu_g, sgu_w, sgu_bias, pool_w_bd, pool_scale, ret_g, decay, zeta, xi, cdec)


def _dilated_attn_kernel(q_ref, k_ref, v_ref, ga_ref, o_ref, bias_s, ob_s, lb_s, kc_s, vc_s):
    bi = pl.program_id(0)
    qb = pl.program_id(1)
    base = qb * ATT_TILE
    lane_head = _lane_head()

    @pl.when(qb == 0)
    def _():
        kc_s[...] = jnp.zeros_like(kc_s)
        vc_s[...] = jnp.zeros_like(vc_s)

    @pl.when((bi == 0) & (qb == 0))
    def _():
        ki = lax.broadcasted_iota(jnp.int32, (2 * CHUNK, CHUNK), 0)
        qi = lax.broadcasted_iota(jnp.int32, (2 * CHUNK, CHUNK), 1)
        dist = qi + CHUNK - ki
        band = (dist >= 0) & (dist <= CHUNK)
        distf = dist.astype(F32)
        for di, dil in enumerate(DILATIONS):
            for hh in range(N_HEADS):
                slope = 2.0 ** (-8.0 * (hh + 1) / N_HEADS)
                bias = -(slope * dil * LOG2E) * distf
                cs = pl.ds(hh * CHUNK, CHUNK)
                bias_s[0, di, :, cs] = jnp.where(band, bias, -jnp.inf)
                bias_s[1, di, :, cs] = jnp.where(band & (ki >= CHUNK), bias, -jnp.inf)

    def rows(ref, start, stride):
        idx = pl.ds(start, CHUNK) if stride == 1 else pl.ds(start, CHUNK, stride=stride)
        return jnp.concatenate([ref[0, s, idx, :] for s in range(N_SLABS)], axis=-1).astype(BF16)

    def key_block(di, start):
        dil = DILATIONS[di]
        return rows(k_ref, start, dil), rows(v_ref, start, dil).T

    def attn_scores(di, q_start, q_local, first, prev_keys):
        dil = DILATIONS[di]
        span = dil * CHUNK
        cur = q_start
        fi = first.astype(jnp.int32)
        if prev_keys is None:
            prev_keys = key_block(di, jnp.where(first, cur, cur - span))
        cur_keys = key_block(di, cur)
        qv = rows(q_ref, q_local, dil)
        kk = jnp.concatenate([prev_keys[0], cur_keys[0]], axis=0)
        v_t = jnp.concatenate([prev_keys[1], cur_keys[1]], axis=1)
        qs = jnp.concatenate(
            [jnp.where(lane_head == hh, qv, jnp.zeros_like(qv)) for hh in range(N_HEADS)], axis=0)
        sc = _dot_nt(kk, qs) + bias_s[fi, di]
        return (sc, v_t), cur_keys

    def attn_values(sc, v_t):
        m = jnp.max(sc, axis=0, keepdims=True)
        pb = jnp.exp2(sc - m).astype(BF16)
        ones = jnp.ones((ONES_ROWS, 2 * CHUNK), BF16)
        o_parts, lse_parts = [], []
        for hh in range(N_HEADS):
            cs = slice(hh * CHUNK, (hh + 1) * CHUNK)
            v_ext = jnp.concatenate([v_t[hh * HEAD_DIM:(hh + 1) * HEAD_DIM, :], ones], axis=0)
            oh = _dot(v_ext, pb[:, cs])
            l = oh[HEAD_DIM:HEAD_DIM + 1, :]
            o_parts.append(oh[0:HEAD_DIM, :] * (1.0 / l))
            ls = m[:, cs] + jnp.log2(l)
            lse_parts.append(jnp.broadcast_to(ls, (HEAD_DIM, CHUNK)))
        o_t = jnp.concatenate(o_parts, axis=0)
        lse_t = jnp.concatenate(lse_parts, axis=0)
        return o_t.T, lse_t.T

    def finish_strided(di, slot, q_local, staged):
        o, lse = attn_values(*staged)
        idx = pl.ds(q_local, CHUNK, stride=DILATIONS[di])
        for s in range(N_SLABS):
            ob_s[slot, s, idx, :] = o[:, s * LANE:(s + 1) * LANE]
            lb_s[slot, s, idx, :] = lse[:, s * LANE:(s + 1) * LANE]

    def finish_dense(di, slot, q_local, staged):
        o1, l1 = attn_values(*staged)
        idx = pl.ds(q_local, CHUNK)
        o4 = jnp.concatenate([ob_s[0, s, idx, :] for s in range(N_SLABS)], axis=-1)
        l4 = jnp.concatenate([lb_s[0, s, idx, :] for s in range(N_SLABS)], axis=-1)
        o16 = jnp.concatenate([ob_s[1, s, idx, :] for s in range(N_SLABS)], axis=-1)
        l16 = jnp.concatenate([lb_s[1, s, idx, :] for s in range(N_SLABS)], axis=-1)
        mx = jnp.maximum(jnp.maximum(l1, l4), l16)
        w1 = jnp.exp2(l1 - mx)
        w4 = jnp.exp2(l4 - mx)
        w16 = jnp.exp2(l16 - mx)
        o = (w1 * o1 + w4 * o4 + w16 * o16) / (w1 + w4 + w16)
        o_ref[0, idx, :] = (o * ga_ref[0, idx, :].astype(F32)).astype(BF16)

    def run_branch(di, slot, finish):
        span = DILATIONS[di] * CHUNK
        per_class = ATT_TILE // span

        def group(g, carry):
            in_flight = []
            keys = None
            for u in range(ATT_GROUP):
                i = g * ATT_GROUP + u
                nb = i % per_class
                q_local = i // per_class + nb * span
                if di == 0:
                    q_local = pl.multiple_of(q_local, CHUNK)
                first = (qb == 0) & (nb == 0)
                if per_class == 1:
                    keys = (kc_s[i], vc_s[i])
                elif u % min(per_class, ATT_GROUP) == 0:
                    keys = None
                staged, keys = attn_scores(di, base + q_local, q_local, first, keys)
                if per_class == 1:
                    kc_s[i], vc_s[i] = keys
                in_flight.append((q_local, staged))
                if len(in_flight) > ATT_LOOKAHEAD:
                    finish(di, slot, *in_flight.pop(0))
            for args in in_flight:
                finish(di, slot, *args)
            return carry

        lax.fori_loop(0, ATT_TILE // CHUNK // ATT_GROUP, group, 0)

    run_branch(1, 0, finish_strided)
    run_branch(2, 1, finish_strided)
    run_branch(0, None, finish_dense)


def _dilated_attn(q, k, v, ga):
    b, _, s, _ = q.shape
    gw = GROUP_WIDTH
    n_branch = len(DILATIONS)
    return pl.pallas_call(
        _dilated_attn_kernel,
        grid=(b, s // ATT_TILE),
        in_specs=[
            pl.BlockSpec((1, N_SLABS, ATT_TILE, LANE), lambda bi, qi: (bi, 0, qi, 0)),
            pl.BlockSpec((1, N_SLABS, s, LANE), lambda bi, qi: (bi, 0, 0, 0)),
            pl.BlockSpec((1, N_SLABS, s, LANE), lambda bi, qi: (bi, 0, 0, 0)),
            pl.BlockSpec((1, ATT_TILE, gw), lambda bi, qi: (bi, qi, 0)),
        ],
        out_specs=pl.BlockSpec((1, ATT_TILE, gw), lambda bi, qi: (bi, qi, 0)),
        out_shape=jax.ShapeDtypeStruct((b, s, gw), BF16),
        scratch_shapes=[
            pltpu.VMEM((2, n_branch, 2 * CHUNK, N_HEADS * CHUNK), F32),
            pltpu.VMEM((n_branch - 1, N_SLABS, ATT_TILE, LANE), F32),
            pltpu.VMEM((n_branch - 1, N_SLABS, ATT_TILE, LANE), F32),
            pltpu.VMEM((DILATIONS[-1], CHUNK, gw), BF16),
            pltpu.VMEM((DILATIONS[-1], gw, CHUNK), BF16),
        ],
        compiler_params=pltpu.CompilerParams(
            dimension_semantics=("arbitrary", "arbitrary"), vmem_limit_bytes=VMEM_LIMIT),
        name="dilated_attn",
    )(q, k, v, ga)


def _outproj_kernel(layer, x_ref, ya_ref, y_ref, wout_ref, postg_ref, o_ref):
    gw = GROUP_WIDTH
    w = wout_ref[...].astype(BF16)
    y = _dot(ya_ref[0], w[0:gw, :]) + _dot(y_ref[0], w[gw:, :])
    ms = jnp.mean(y * y, axis=-1, keepdims=True)
    o_ref[0] = x_ref[0] + y * lax.rsqrt(ms + NORM_EPS) * postg_ref[layer:layer + 1, :]


def _outproj(layer, x, ya, ybcd, w_out, post_g):
    b, s, d = x.shape
    tile = OUT_TILE
    gw = GROUP_WIDTH
    return pl.pallas_call(
        functools.partial(_outproj_kernel, layer),
        grid=(b, s // tile),
        in_specs=[
            pl.BlockSpec((1, tile, d), lambda bi, ni: (bi, ni, 0)),
            pl.BlockSpec((1, tile, gw), lambda bi, ni: (bi, ni, 0)),
            pl.BlockSpec((1, tile, 3 * gw), lambda bi, ni: (bi, ni, 0)),
            _layer_spec(layer, w_out.shape[1:]),
            pl.BlockSpec(post_g.shape, lambda bi, ni: (0, 0)),
        ],
        out_specs=pl.BlockSpec((1, tile, d), lambda bi, ni: (bi, ni, 0)),
        out_shape=jax.ShapeDtypeStruct(x.shape, x.dtype),
        compiler_params=pltpu.CompilerParams(
            dimension_semantics=("arbitrary", "arbitrary"), vmem_limit_bytes=VMEM_LIMIT),
        name="outproj",
    )(x, ya, ybcd, w_out, post_g)


def _retention_tables():
    f32 = np.float32
    log_g = np.log(f32(1.0) - np.exp2(f32(-5.0) - np.arange(N_HEADS, dtype=f32)))
    i = np.arange(CHUNK, dtype=f32)
    diff = i[:, None] - i[None, :]
    decay = np.where(diff >= 0, np.exp(log_g[:, None, None] * np.maximum(diff, f32(0.0))), f32(0.0))
    zeta = np.exp(log_g[:, None] * (f32(CHUNK - 1) - i))
    xi = np.exp(log_g[:, None] * (i + f32(1.0)))
    cdec = np.exp(log_g * f32(CHUNK))
    per_lane = lambda t: np.repeat(t.T, HEAD_DIM, axis=1)
    decay = decay.transpose(1, 0, 2).reshape(CHUNK, N_HEADS * CHUNK)
    tables = decay, per_lane(zeta), per_lane(xi), np.repeat(cdec, HEAD_DIM)[None, :]
    return tuple(jnp.asarray(t, F32) for t in tables)


def kernel(x, pre_g, w_in, sgu_g, sgu_w, sgu_b, pool_w, pool_scale, ret_g, w_out, post_g):
    sgu_bias = jnp.repeat(jnp.swapaxes(sgu_b, 1, 2), HEAD_DIM, axis=2)
    eye = jnp.eye(len(POOL_SIZES), dtype=pool_w.dtype)
    pool_w_bd = jnp.einsum('lgcd,gh->lgchd', pool_w, eye).reshape(
        pool_w.shape[0], GROUP_WIDTH, GROUP_WIDTH).astype(BF16)
    tables = _retention_tables()
    for layer in range(pre_g.shape[0]):
        q, k, v, ga, ybcd = _inproj_mix(
            layer, x, pre_g, w_in, sgu_g, sgu_w, sgu_bias, pool_w_bd, pool_scale, ret_g, *tables)
        ya = _dilated_attn(q, k, v, ga)
        x = _outproj(layer, x, ya, ybcd, w_out, post_g)
    return x
```
